```python
import math
import jax, jax.numpy as jnp
from jax import lax
import numpy as np

D_MODEL = 1024
BATCH = 4
SEQ = 4096
DEPTH = 2

N_MIXERS = 2
MEM_LEN = 256
GDN_HEADS = 8
GDN_HEAD_DIM = D_MODEL // GDN_HEADS
GDN_WIDTH = GDN_HEADS * GDN_HEAD_DIM
CONV_WIDTH = 4
CHUNK = 64
DIFF_HEADS = 8
DIFF_HEAD_DIM = D_MODEL // (2 * DIFF_HEADS)
DIFF_WIDTH = 2 * DIFF_HEADS * DIFF_HEAD_DIM
Q_BLOCK = 128
MEM_HEADS = 4
MEM_HEAD_DIM = 128
MEM_WIDTH = MEM_HEADS * MEM_HEAD_DIM
D_FF = 4 * D_MODEL
ALPHA = (2.0 * DEPTH) ** 0.25
BETA_INIT = (8.0 * DEPTH) ** -0.25
LN_EPS = 1e-5
RMS_EPS = 1e-6
GDN_IN = 4 * GDN_WIDTH + 2 * GDN_HEADS + MEM_WIDTH
DIFF_IN = 3 * DIFF_WIDTH + MEM_WIDTH
GDN_CAT = GDN_WIDTH + MEM_WIDTH
DIFF_CAT = DIFF_WIDTH + MEM_WIDTH

kernel_name = 'hybrid_gdn_diffattn_memxattn_deepnorm'


def layer_norm(x, g, b):
    xf = x.astype(jnp.float32)
    mu = xf.mean(-1, keepdims=True)
    var = jnp.square(xf - mu).mean(-1, keepdims=True)
    return ((xf - mu) * lax.rsqrt(var + LN_EPS) * g + b).astype(x.dtype)


def rms_norm(x, w):
    xf = x.astype(jnp.float32)
    return (xf * lax.rsqrt(jnp.square(xf).mean(-1, keepdims=True) + RMS_EPS) * w).astype(x.dtype)


def l2_normalize(x):
    xf = x.astype(jnp.float32)
    return (xf * lax.rsqrt(jnp.square(xf).sum(-1, keepdims=True) + RMS_EPS)).astype(x.dtype)


def causal_depthwise_conv(x, w):
    k = w.shape[0]
    return lax.conv_general_dilated(x, w[:, None, :].astype(x.dtype), window_strides=(1,),
                                    padding=[(k - 1, 0)], dimension_numbers=('NWC', 'WIO', 'NWC'),
                                    feature_group_count=x.shape[-1])


def chunk_gated_delta_rule(q, k, v, g, beta):
    B, T, H, Dk = q.shape
    n = T // CHUNK
    f32 = jnp.float32

    def to_chunks(a):
        return a.astype(f32).reshape(B, n, CHUNK, H, *a.shape[3:]).swapaxes(2, 3)

    qc = to_chunks(q) * (Dk ** -0.5)
    kc = to_chunks(k)
    vc = to_chunks(v)
    gc = jnp.cumsum(to_chunks(g), axis=-1)
    bc = to_chunks(beta)
    k_beta = kc * bc[..., None]
    v_beta = vc * bc[..., None]
    causal = jnp.tril(jnp.ones((CHUNK, CHUNK), bool))
    strict = jnp.tril(jnp.ones((CHUNK, CHUNK), bool), -1)
    decay = jnp.exp(jnp.where(causal, gc[..., :, None] - gc[..., None, :], -jnp.inf))
    lower = jnp.where(strict, jnp.einsum('bnhcd,bnhsd->bnhcs', k_beta, kc) * decay, 0.0)
    a_mat = jnp.eye(CHUNK, dtype=f32) + lower
    u = lax.linalg.triangular_solve(a_mat, v_beta, left_side=True, lower=True, unit_diagonal=True)
    w = lax.linalg.triangular_solve(a_mat, k_beta * jnp.exp(gc)[..., None], left_side=True,
                                    lower=True, unit_diagonal=True)
    attn_qk = jnp.einsum('bnhcd,bnhsd->bnhcs', qc, kc) * decay
    q_decay = qc * jnp.exp(gc)[..., None]
    k_decay = kc * jnp.exp(gc[..., -1:] - gc)[..., None]
    chunk_decay = jnp.exp(gc[..., -1])
    xs = tuple(jnp.moveaxis(a, 1, 0) for a in (q_decay, k_decay, u, w, attn_qk, chunk_decay))

    def step(state, inp):
        qd, kd, u_c, w_c, a_c, dc = inp
        v_new = u_c - jnp.einsum('bhcd,bhde->bhce', w_c, state)
        o = jnp.einsum('bhcd,bhde->bhce', qd, state) + jnp.einsum('bhcs,bhse->bhce', a_c, v_new)
        state = state * dc[..., None, None] + jnp.einsum('bhcd,bhce->bhde', kd, v_new)
        return state, o

    s0 = jnp.zeros((B, H, Dk, v.shape[-1]), f32)
    _, o = lax.scan(step, s0, xs)
    return o.transpose(1, 0, 3, 2, 4).reshape(B, T, H, v.shape[-1]).astype(q.dtype)


def differential_attention(q, k, v, lam):
    B, T, H, _, d = q.shape
    nb = T // Q_BLOCK
    qb = q.reshape(B, nb, Q_BLOCK, H, 2, d).swapaxes(0, 1)
    key_pos = jnp.arange(T)
    scale = d ** -0.5

    def block(args):
        q_blk, i = args
        s = jnp.einsum('bqhmd,bkhmd->bhmqk', q_blk, k).astype(jnp.float32) * scale
        q_pos = i * Q_BLOCK + jnp.arange(Q_BLOCK)
        s = jnp.where(key_pos[None, :] <= q_pos[:, None], s, -jnp.inf)
        p = jax.nn.softmax(s, axis=-1)
        a = p[:, :, 0] - lam * p[:, :, 1]
        return jnp.einsum('bhqk,bkhe->bqhe', a.astype(v.dtype), v)

    o = lax.map(block, (qb, jnp.arange(nb)))
    return o.swapaxes(0, 1).reshape(B, T, H, 2 * d)


def memory_attention(mq, mem_k, mem_v):
    B, T, _ = mq.shape
    q = mq.reshape(B, T, MEM_HEADS, MEM_HEAD_DIM)
    s = jnp.einsum('bthd,bmhd->bhtm', q, mem_k).astype(jnp.float32) * (MEM_HEAD_DIM ** -0.5)
    p = jax.nn.softmax(s, axis=-1)
    return jnp.einsum('bhtm,bmhd->bthd', p.astype(mem_v.dtype), mem_v).reshape(B, T, MEM_WIDTH)


def sqrelu_mlp(x, w1, w2):
    return jnp.square(jax.nn.relu(x @ w1)) @ w2


def lambda_init(layer_idx):
    return 0.8 - 0.6 * math.exp(-0.3 * layer_idx)


def gdn_mixer(x, mem_k, mem_v, w_in, conv_w, a_log, dt_bias, gate_norm_w, w_out):
    B, T, _ = x.shape
    proj = x @ w_in
    qkv = proj[..., :3 * GDN_WIDTH]
    z = proj[..., 3 * GDN_WIDTH:4 * GDN_WIDTH]
    a = proj[..., 4 * GDN_WIDTH:4 * GDN_WIDTH + GDN_HEADS]
    b = proj[..., 4 * GDN_WIDTH + GDN_HEADS:4 * GDN_WIDTH + 2 * GDN_HEADS]
    mq = proj[..., 4 * GDN_WIDTH + 2 * GDN_HEADS:]
    qkv = jax.nn.silu(causal_depthwise_conv(qkv, conv_w))
    heads = lambda t: t.reshape(B, T, GDN_HEADS, GDN_HEAD_DIM)
    q = l2_normalize(heads(qkv[..., :GDN_WIDTH]))
    k = l2_normalize(heads(qkv[..., GDN_WIDTH:2 * GDN_WIDTH]))
    v = heads(qkv[..., 2 * GDN_WIDTH:])
    beta = jax.nn.sigmoid(b.astype(jnp.float32))
    g = -jnp.exp(a_log) * jax.nn.softplus(a.astype(jnp.float32) + dt_bias)
    o = chunk_gated_delta_rule(q, k, v, g, beta)
    o = (rms_norm(o, gate_norm_w) * jax.nn.silu(heads(z))).reshape(B, T, GDN_WIDTH)
    m = memory_attention(mq, mem_k, mem_v)
    return jnp.concatenate([o, m], axis=-1) @ w_out


def diff_mixer(x, mem_k, mem_v, layer_idx, w_in, lq1, lk1, lq2, lk2, subln_w, w_out):
    B, T, _ = x.shape
    proj = x @ w_in
    q = proj[..., :DIFF_WIDTH].reshape(B, T, DIFF_HEADS, 2, DIFF_HEAD_DIM)
    k = proj[..., DIFF_WIDTH:2 * DIFF_WIDTH].reshape(B, T, DIFF_HEADS, 2, DIFF_HEAD_DIM)
    v = proj[..., 2 * DIFF_WIDTH:3 * DIFF_WIDTH].reshape(B, T, DIFF_HEADS, 2 * DIFF_HEAD_DIM)
    mq = proj[..., 3 * DIFF_WIDTH:]
    lam_init = lambda_init(layer_idx)
    f32 = jnp.float32
    lam = (jnp.exp(jnp.sum(lq1.astype(f32) * lk1.astype(f32)))
           - jnp.exp(jnp.sum(lq2.astype(f32) * lk2.astype(f32))) + lam_init)
    o = differential_attention(q, k, v, lam)
    o = (rms_norm(o, subln_w) * (1.0 - lam_init)).reshape(B, T, DIFF_WIDTH)
    m = memory_attention(mq, mem_k, mem_v)
    return jnp.concatenate([o, m], axis=-1) @ w_out


def setup_inputs(seed: int = 0) -> dict:
    key = jax.random.key(seed)
    ks = iter(jax.random.split(key, 40))
    nrm = lambda shape, s: jax.random.normal(next(ks), shape, jnp.float32) * s
    gain = lambda n: 1.0 + nrm((n,), 0.02)
    bias = lambda n: nrm((n,), 0.02)
    d = D_MODEL
    inp = {}
    inp['x'] = nrm((BATCH, SEQ, d), 1.0)
    inp['mem'] = nrm((BATCH, MEM_LEN, d), 1.0)
    inp['mem_ln_g'] = gain(d)
    inp['mem_ln_b'] = bias(d)
    inp['w_mem_kv'] = nrm((d, 2 * MEM_WIDTH), d ** -0.5)
    inp['l0_w_in'] = nrm((d, GDN_IN), d ** -0.5)
    inp['l0_conv_w'] = nrm((CONV_WIDTH, 3 * GDN_WIDTH), CONV_WIDTH ** -0.5)
    inp['l0_a_log'] = jnp.log(jax.random.uniform(next(ks), (GDN_HEADS,), jnp.float32, 1.0, 16.0))
    dt = jnp.exp(jax.random.uniform(next(ks), (GDN_HEADS,), jnp.float32, math.log(1e-3), math.log(1e-1)))
    inp['l0_dt_bias'] = dt + jnp.log(-jnp.expm1(-dt))
    inp['l0_gate_norm_w'] = gain(GDN_HEAD_DIM)
    inp['l0_w_out'] = nrm((GDN_CAT, d), GDN_CAT ** -0.5 * BETA_INIT)
    inp['l0_ln1_g'] = gain(d)
    inp['l0_ln1_b'] = bias(d)
    inp['l0_w_ff1'] = nrm((d, D_FF), d ** -0.5)
    inp['l0_w_ff2'] = nrm((D_FF, d), D_FF ** -0.5 * BETA_INIT)
    inp['l0_ln2_g'] = gain(d)
    inp['l0_ln2_b'] = bias(d)
    inp['l1_w_in'] = nrm((d, DIFF_IN), d ** -0.5)
    inp['l1_lambda_q1'] = nrm((DIFF_HEAD_DIM,), 0.1)
    inp['l1_lambda_k1'] = nrm((DIFF_HEAD_DIM,), 0.1)
    inp['l1_lambda_q2'] = nrm((DIFF_HEAD_DIM,), 0.1)
    inp['l1_lambda_k2'] = nrm((DIFF_HEAD_DIM,), 0.1)
    inp['l1_subln_w'] = gain(2 * DIFF_HEAD_DIM)
    inp['l1_w_out'] = nrm((DIFF_CAT, d), DIFF_CAT ** -0.5 * BETA_INIT)
    inp['l1_ln1_g'] = gain(d)
    inp['l1_ln1_b'] = bias(d)
    inp['l1_w_ff1'] = nrm((d, D_FF), d ** -0.5)
    inp['l1_w_ff2'] = nrm((D_FF, d), D_FF ** -0.5 * BETA_INIT)
    inp['l1_ln2_g'] = gain(d)
    inp['l1_ln2_b'] = bias(d)
    return inp


def reference(x, mem, mem_ln_g, mem_ln_b, w_mem_kv,
              l0_w_in, l0_conv_w, l0_a_log, l0_dt_bias, l0_gate_norm_w, l0_w_out,
              l0_ln1_g, l0_ln1_b, l0_w_ff1, l0_w_ff2, l0_ln2_g, l0_ln2_b,
              l1_w_in, l1_lambda_q1, l1_lambda_k1, l1_lambda_q2, l1_lambda_k2, l1_subln_w, l1_w_out,
              l1_ln1_g, l1_ln1_b, l1_w_ff1, l1_w_ff2, l1_ln2_g, l1_ln2_b):
    B = mem.shape[0]
    mem_kv = layer_norm(mem, mem_ln_g, mem_ln_b) @ w_mem_kv
    mem_k = mem_kv[..., :MEM_WIDTH].reshape(B, MEM_LEN, MEM_HEADS, MEM_HEAD_DIM)
    mem_v = mem_kv[..., MEM_WIDTH:].reshape(B, MEM_LEN, MEM_HEADS, MEM_HEAD_DIM)

    mixer_params = [
        (l0_w_in, l0_conv_w, l0_a_log, l0_dt_bias, l0_gate_norm_w, l0_w_out),
        (l1_w_in, l1_lambda_q1, l1_lambda_k1, l1_lambda_q2, l1_lambda_k2, l1_subln_w, l1_w_out),
    ]
    norm_ffn_params = [
        (l0_ln1_g, l0_ln1_b, l0_w_ff1, l0_w_ff2, l0_ln2_g, l0_ln2_b),
        (l1_ln1_g, l1_ln1_b, l1_w_ff1, l1_w_ff2, l1_ln2_g, l1_ln2_b),
    ]
    for i in range(DEPTH):
        if i % N_MIXERS == 0:
            mix = gdn_mixer(x, mem_k, mem_v, *mixer_params[i])
        else:
            mix = diff_mixer(x, mem_k, mem_v, i, *mixer_params[i])
        ln1_g, ln1_b, w_ff1, w_ff2, ln2_g, ln2_b = norm_ffn_params[i]
        x = layer_norm(ALPHA * x + mix, ln1_g, ln1_b)
        x = layer_norm(ALPHA * x + sqrelu_mlp(x, w_ff1, w_ff2), ln2_g, ln2_b)
    return x
```

```python
import functools
import math

import jax
import jax.numpy as jnp
from jax import lax
from jax.experimental import pallas as pl
from jax.experimental.pallas import tpu as pltpu

F32 = jnp.float32
BF16 = jnp.bfloat16

DEPTH = 2
GDN_HEADS = 8
GDN_HEAD_DIM = 128
CONV_WIDTH = 4
GDN_CHUNK = 64
DIFF_HEADS = 8
DIFF_HEAD_DIM = 64
MEM_HEADS = 4
MEM_HEAD_DIM = 128
ALPHA = (2.0 * DEPTH) ** 0.25
LN_EPS = 1e-5
RMS_EPS = 1e-6

LANES = 128
SUBLANES = 8
VMEM_LIMIT_BYTES = 56 * 1024 * 1024

NEG_BIG = -1e30


def _params(n_axes):
    return pltpu.CompilerParams(
        dimension_semantics=("arbitrary",) * n_axes,
        vmem_limit_bytes=VMEM_LIMIT_BYTES,
    )


def _resident(shape):
    nd = len(shape)
    return pl.BlockSpec(shape, lambda *_: (0,) * nd, pipeline_mode=pl.Buffered(1))


def _dot(a, b):
    return jnp.dot(a, b, preferred_element_type=F32)


def _dot_nt(a, b):
    return lax.dot_general(a, b, (((1,), (1,)), ((), ())), preferred_element_type=F32)


def _dot_tn(a, b):
    return lax.dot_general(a, b, (((0,), (0,)), ((), ())), preferred_element_type=F32)


def _layer_norm(y, g, b):
    mu = jnp.mean(y, axis=-1, keepdims=True)
    yc = y - mu
    var = jnp.mean(yc * yc, axis=-1, keepdims=True)
    return yc * lax.rsqrt(var + LN_EPS) * g + b


def _silu(x):
    return x * jax.nn.sigmoid(x)


def _softplus(x):
    return jnp.maximum(x, 0.0) + jnp.log1p(jnp.exp(-jnp.abs(x)))


def _mem_kv_kernel(mem_ref, g_ref, b_ref, w_ref, o_ref):
    y = _layer_norm(mem_ref[...], g_ref[...], b_ref[...])
    o_ref[...] = _dot(y.astype(BF16), w_ref[...]).astype(o_ref.dtype)


def _mem_kv(mem2d, g, b, w, rows):
    n, d = mem2d.shape
    nout = w.shape[1]
    return pl.pallas_call(
        _mem_kv_kernel,
        grid=(n // rows,),
        in_specs=[
            pl.BlockSpec((rows, d), lambda i: (i, 0)),
            _resident((1, d)),
            _resident((1, d)),
            _resident((d, nout)),
        ],
        out_specs=pl.BlockSpec((rows, nout), lambda i: (i, 0)),
        out_shape=jax.ShapeDtypeStruct((n, nout), BF16),
        compiler_params=_params(1),
        name="mem_kv",
    )(mem2d, g, b, w)


def _in_proj_kernel(x_ref, w_ref, o_ref, *, col_chunk):
    xb = x_ref[...].astype(BF16)
    nout = o_ref.shape[1]
    for c0 in range(0, nout, col_chunk):
        c1 = min(c0 + col_chunk, nout)
        o_ref[:, c0:c1] = _dot(xb, w_ref[:, c0:c1]).astype(o_ref.dtype)


def _in_proj(x2d, w, out_dtype, tm=512, col_chunk=512):
    n, d = x2d.shape
    nout = w.shape[1]
    return pl.pallas_call(
        functools.partial(_in_proj_kernel, col_chunk=col_chunk),
        grid=(n // tm,),
        in_specs=[pl.BlockSpec((tm, d), lambda i: (i, 0)), _resident((d, nout))],
        out_specs=pl.BlockSpec((tm, nout), lambda i: (i, 0)),
        out_shape=jax.ShapeDtypeStruct((n, nout), out_dtype),
        compiler_params=_params(1),
        name="in_proj",
    )(x2d, w)


def _gdn_kernel(qkv_ref, z_ref, ab_ref, cw_ref, hp_ref, gw_ref, o_ref, s_ref, tail_ref,
                *, heads, hd, chunk):
    @pl.when(pl.program_id(1) == 0)
    def _():
        s_ref[...] = jnp.zeros_like(s_ref)
        tail_ref[...] = jnp.zeros_like(tail_ref)

    rows = qkv_ref.shape[0]
    width = heads * hd
    cw = cw_ref[...]
    neg_a = -jnp.exp(hp_ref[0:1, :])
    dt_bias = hp_ref[1:2, :]
    gate_w = gw_ref[...]
    scale = hd ** -0.5

    ri = lax.broadcasted_iota(jnp.int32, (chunk, chunk), 0)
    ci = lax.broadcasted_iota(jnp.int32, (chunk, chunk), 1)
    causal = ri >= ci
    strict = ri > ci
    eye = ri == ci
    tri = causal.astype(F32)
    eye_f = eye.astype(F32)

    def chunk_body(c, carry):
        r0 = pl.multiple_of(c * chunk, chunk)
        xc = qkv_ref[pl.ds(r0, chunk), :]
        ext = jnp.concatenate([tail_ref[...], xc], axis=0)
        conv = cw[CONV_WIDTH - 1:CONV_WIDTH] * xc
        for j in range(CONV_WIDTH - 1):
            off = SUBLANES - (CONV_WIDTH - 1) + j
            conv = conv + cw[j:j + 1] * ext[off:off + chunk]
        tail_ref[...] = xc[chunk - SUBLANES:chunk]
        act = _silu(conv)

        ab = ab_ref[pl.ds(r0, chunk), :]
        g_all = neg_a * _softplus(ab + dt_bias)
        beta_all = jax.nn.sigmoid(ab)
        gc_all = jnp.dot(tri, g_all, precision=lax.Precision.HIGHEST,
                         preferred_element_type=F32)
        zc = z_ref[pl.ds(r0, chunk), :]

        outs = []
        for h in range(heads):
            qh = act[:, h * hd:(h + 1) * hd]
            kh = act[:, width + h * hd:width + (h + 1) * hd]
            vh = act[:, 2 * width + h * hd:2 * width + (h + 1) * hd]
            qn = qh * lax.rsqrt(jnp.sum(qh * qh, axis=-1, keepdims=True) + RMS_EPS)
            kn = kh * lax.rsqrt(jnp.sum(kh * kh, axis=-1, keepdims=True) + RMS_EPS)
            gcol = gc_all[:, h:h + 1]
            bcol = beta_all[:, heads + h:heads + h + 1]
            eg = jnp.exp(gcol)
            g_i = jnp.broadcast_to(gcol, (chunk, chunk))
            g_j = jnp.sum(jnp.where(eye, g_i, 0.0), axis=0, keepdims=True)
            decay = jnp.where(causal, jnp.exp(jnp.where(causal, g_i - g_j, 0.0)), 0.0)
            g_last = gcol[chunk - 1:chunk, :]

            kb = kn * bcol
            vb = vh * bcol
            qs = qn * scale
            r1 = _dot_nt(jnp.concatenate([kb, qs], axis=0).astype(BF16), kn.astype(BF16))
            low = jnp.where(strict, r1[:chunk] * decay, 0.0)
            attn = r1[chunk:] * decay

            x = eye_f - low
            lb = low.astype(BF16)
            m = _dot(lb, lb)
            for _ in range(4):
                mb = m.astype(BF16)
                r2 = _dot(jnp.concatenate([x, m], axis=0).astype(BF16), mb)
                x = x + r2[:chunk]
                m = r2[chunk:]
            x = x + _dot(x.astype(BF16), m.astype(BF16))

            uw = _dot(x.astype(BF16), jnp.concatenate([vb, kb * eg], axis=1).astype(BF16))
            u = uw[:, :hd]
            w = uw[:, hd:]

            state = s_ref[h]
            r3 = _dot(jnp.concatenate([w, qs * eg], axis=0).astype(BF16), state.astype(BF16))
            v_new = u - r3[:chunk]
            vnb = v_new.astype(BF16)
            o = r3[chunk:] + _dot(attn.astype(BF16), vnb)
            k_dec = kn * jnp.exp(g_last - gcol)
            s_ref[h] = state * jnp.exp(g_last) + _dot_tn(k_dec.astype(BF16), vnb)

            on = o * lax.rsqrt(jnp.mean(o * o, axis=-1, keepdims=True) + RMS_EPS) * gate_w
            outs.append(on * _silu(zc[:, h * hd:(h + 1) * hd]))

        o_ref[pl.ds(r0, chunk), :] = jnp.concatenate(outs, axis=-1).astype(o_ref.dtype)
        return carry

    lax.fori_loop(0, rows // chunk, chunk_body, 0)


def _gdn(proj, conv_w, head_params, gate_w, batch, seq, rows=512):
    heads, hd = GDN_HEADS, GDN_HEAD_DIM
    width = heads * hd
    n = batch * seq
    nr = seq // rows
    row_map = lambda b, i: b * nr + i
    ab_block = (4 * width + MEM_HEADS * MEM_HEAD_DIM) // LANES
    return pl.pallas_call(
        functools.partial(_gdn_kernel, heads=heads, hd=hd, chunk=GDN_CHUNK),
        grid=(batch, nr),
        in_specs=[
            pl.BlockSpec((rows, 3 * width), lambda b, i: (row_map(b, i), 0)),
            pl.BlockSpec((rows, width), lambda b, i: (row_map(b, i), 3)),
            pl.BlockSpec((rows, LANES), lambda b, i: (row_map(b, i), ab_block)),
            _resident((CONV_WIDTH, 3 * width)),
            _resident((2, LANES)),
            _resident((1, hd)),
        ],
        out_specs=pl.BlockSpec((rows, width), lambda b, i: (row_map(b, i), 0)),
        out_shape=jax.ShapeDtypeStruct((n, width), BF16),
        scratch_shapes=[
            pltpu.VMEM((heads, hd, hd), F32),
            pltpu.VMEM((SUBLANES, 3 * width), F32),
        ],
        compiler_params=_params(2),
        name="gdn",
    )(proj, proj, proj, conv_w, head_params, gate_w)


def _diff_attn_kernel(q_ref, k_ref, v_ref, lp_ref, sw_ref, o_ref, *, blk, lam_init):
    i = pl.program_id(2)
    d = DIFF_HEAD_DIM
    q = q_ref[...]
    lane = lax.broadcasted_iota(jnp.int32, q.shape, 1)
    zero = jnp.zeros_like(q)
    qs = jnp.concatenate([jnp.where(lane < d, q, zero), jnp.where(lane >= d, q, zero)],
                         axis=0) * (d ** -0.5)

    def step(j, carry, masked):
        m, l, acc = carry
        r0 = pl.multiple_of(j * blk, blk)
        kb = k_ref[pl.ds(r0, blk), :]
        vb = v_ref[pl.ds(r0, blk), :]
        s = _dot_nt(qs, kb)
        if masked:
            rr = lax.broadcasted_iota(jnp.int32, s.shape, 0)
            cc = lax.broadcasted_iota(jnp.int32, s.shape, 1)
            rr = jnp.where(rr >= blk, rr - blk, rr)
            s = jnp.where(cc <= rr, s, NEG_BIG)
        m_new = jnp.maximum(m, jnp.max(s, axis=-1, keepdims=True))
        a = jnp.exp(m - m_new)
        p = jnp.exp(s - m_new)
        l = a * l + jnp.sum(p, axis=-1, keepdims=True)
        acc = a * acc + _dot(p.astype(BF16), vb)
        return m_new, l, acc

    init = (jnp.full((2 * blk, 1), NEG_BIG, F32), jnp.zeros((2 * blk, 1), F32),
            jnp.zeros((2 * blk, 2 * d), F32))
    carry = lax.fori_loop(0, i, lambda j, c: step(j, c, False), init)
    _, l, acc = step(i, carry, True)

    lp = lp_ref[...]
    lam = (jnp.exp(jnp.sum(lp[0:1] * lp[1:2], axis=-1, keepdims=True))
           - jnp.exp(jnp.sum(lp[2:3] * lp[3:4], axis=-1, keepdims=True)) + lam_init)
    o = acc[:blk] / l[:blk] - lam * (acc[blk:] / l[blk:])
    on = o * lax.rsqrt(jnp.mean(o * o, axis=-1, keepdims=True) + RMS_EPS) * sw_ref[...]
    o_ref[...] = (on * (1.0 - lam_init)).astype(o_ref.dtype)


def _diff_attn(proj, lam_params, subln_w, batch, seq, lam_init, blk=256):
    heads = DIFF_HEADS
    hw = 2 * DIFF_HEAD_DIM
    n = batch * seq
    nq = seq // blk
    return pl.pallas_call(
        functools.partial(_diff_attn_kernel, blk=blk, lam_init=lam_init),
        grid=(batch, heads, nq),
        in_specs=[
            pl.BlockSpec((blk, hw), lambda b, h, i: (b * nq + i, h)),
            pl.BlockSpec((seq, hw), lambda b, h, i: (b, heads + h)),
            pl.BlockSpec((seq, hw), lambda b, h, i: (b, 2 * heads + h)),
            _resident((4, hw)),
            _resident((1, hw)),
        ],
        out_specs=pl.BlockSpec((blk, hw), lambda b, h, i: (b * nq + i, h)),
        out_shape=jax.ShapeDtypeStruct((n, heads * hw), BF16),
        compiler_params=_params(3),
        name="diff_attn",
    )(proj, proj, proj, lam_params, subln_w)


def _out_proj_kernel(o_ref, mq_ref, kv_ref, wo_ref, wm_ref, x_ref, g_ref, b_ref, out_ref):
    mq = mq_ref[...].astype(BF16)
    kv = kv_ref[...]
    mw = MEM_HEADS * MEM_HEAD_DIM
    parts = []
    for h in range(MEM_HEADS):
        lo, hi = h * MEM_HEAD_DIM, (h + 1) * MEM_HEAD_DIM
        s = _dot_nt(mq[:, lo:hi], kv[:, lo:hi]) * (MEM_HEAD_DIM ** -0.5)
        p = jnp.exp(s - jnp.max(s, axis=-1, keepdims=True))
        pv = _dot(p.astype(BF16), kv[:, mw + lo:mw + hi])
        parts.append(pv / jnp.sum(p, axis=-1, keepdims=True))
    m = jnp.concatenate(parts, axis=-1).astype(BF16)
    mix = _dot(o_ref[...], wo_ref[...]) + _dot(m, wm_ref[...])
    y = ALPHA * x_ref[...] + mix
    out_ref[...] = _layer_norm(y, g_ref[...], b_ref[...])


def _out_proj(o, proj, mq_block, mem_kv, w_o, w_m, x2d, g, b, seq, mem_len, tm=512):
    n, d = x2d.shape
    mw = MEM_HEADS * MEM_HEAD_DIM
    per_batch = seq // tm
    return pl.pallas_call(
        _out_proj_kernel,
        grid=(n // tm,),
        in_specs=[
            pl.BlockSpec((tm, o.shape[1]), lambda i: (i, 0)),
            pl.BlockSpec((tm, mw), lambda i: (i, mq_block)),
            pl.BlockSpec((mem_len, 2 * mw), lambda i: (i // per_batch, 0)),
            _resident(w_o.shape),
            _resident(w_m.shape),
            pl.BlockSpec((tm, d), lambda i: (i, 0)),
            _resident((1, d)),
            _resident((1, d)),
        ],
        out_specs=pl.BlockSpec((tm, d), lambda i: (i, 0)),
        out_shape=jax.ShapeDtypeStruct((n, d), F32),
        compiler_params=_params(1),
        name="out_proj",
    )(o, proj, mem_kv, w_o, w_m, x2d, g, b)


def _mlp_kernel(x_ref, w1_ref, w2_ref, g_ref, b_ref, out_ref, *, ff_chunk):
    x = x_ref[...]
    xb = x.astype(BF16)
    d_ff = w1_ref.shape[1]
    acc = jnp.zeros(x.shape, F32)
    for c0 in range(0, d_ff, ff_chunk):
        h = jnp.maximum(_dot(xb, w1_ref[:, c0:c0 + ff_chunk]), 0.0)
        acc = acc + _dot((h * h).astype(BF16), w2_ref[c0:c0 + ff_chunk, :])
    out_ref[...] = _layer_norm(ALPHA * x + acc, g_ref[...], b_ref[...])


def _mlp(x2d, w1, w2, g, b, tm=512, ff_chunk=512):
    n, d = x2d.shape
    return pl.pallas_call(
        functools.partial(_mlp_kernel, ff_chunk=ff_chunk),
        grid=(n // tm,),
        in_specs=[
            pl.BlockSpec((tm, d), lambda i: (i, 0)),
            _resident(w1.shape),
            _resident(w2.shape),
            _resident((1, d)),
            _resident((1, d)),
        ],
        out_specs=pl.BlockSpec((tm, d), lambda i: (i, 0)),
        out_shape=jax.ShapeDtypeStruct((n, d), F32),
        compiler_params=_params(1),
        name="mlp",
    )(x2d, w1, w2, g, b)


def _row(v):
    return v.reshape(1, -1).astype(F32)


def _pad_lanes(v, lanes=LANES):
    return jnp.pad(v.astype(F32), (0, lanes - v.shape[0]))


def kernel(x, mem, mem_ln_g, mem_ln_b, w_mem_kv, l0_w_in, l0_conv_w, l0_a_log, l0_dt_bias, l0_gate_norm_w, l0_w_out, l0_ln1_g, l0_ln1_b, l0_w_ff1, l0_w_ff2, l0_ln2_g, l0_ln2_b, l1_w_in, l1_lambda_q1, l1_lambda_k1, l1_lambda_q2, l1_lambda_k2, l1_subln_w, l1_w_out, l1_ln1_g, l1_ln1_b, l1_w_ff1, l1_w_ff2, l1_ln2_g, l1_ln2_b):
    batch, seq, d = x.shape
    mem_len = mem.shape[1]
    n = batch * seq
    gw = GDN_HEADS * GDN_HEAD_DIM
    dw = 2 * DIFF_HEADS * DIFF_HEAD_DIM
    mw = MEM_HEADS * MEM_HEAD_DIM

    x2d = x.reshape(n, d)
    mem_kv = _mem_kv(mem.reshape(batch * mem_len, d), _row(mem_ln_g), _row(mem_ln_b),
                     w_mem_kv.astype(BF16), rows=mem_len)

    ab_cols = l0_w_in[:, 4 * gw:4 * gw + 2 * GDN_HEADS]
    w_in0 = jnp.concatenate(
        [l0_w_in[:, :4 * gw], l0_w_in[:, 4 * gw + 2 * GDN_HEADS:],
         jnp.pad(ab_cols, ((0, 0), (0, LANES - 2 * GDN_HEADS)))], axis=1).astype(BF16)
    proj0 = _in_proj(x2d, w_in0, F32)
    head_params = jnp.stack([_pad_lanes(l0_a_log), _pad_lanes(l0_dt_bias)])
    o0 = _gdn(proj0, l0_conv_w.astype(F32), head_params, _row(l0_gate_norm_w), batch, seq)
    x1 = _out_proj(o0, proj0, (4 * gw) // mw, mem_kv, l0_w_out[:gw].astype(BF16),
                   l0_w_out[gw:].astype(BF16), x2d, _row(l0_ln1_g), _row(l0_ln1_b), seq, mem_len)
    x2 = _mlp(x1, l0_w_ff1.astype(BF16), l0_w_ff2.astype(BF16), _row(l0_ln2_g), _row(l0_ln2_b))

    lam_init = 0.8 - 0.6 * math.exp(-0.3 * 1)
    proj1 = _in_proj(x2, l1_w_in.astype(BF16), BF16)
    lam_params = jnp.stack([_pad_lanes(v) for v in
                            (l1_lambda_q1, l1_lambda_k1, l1_lambda_q2, l1_lambda_k2)])
    o1 = _diff_attn(proj1, lam_params, _row(l1_subln_w), batch, seq, lam_init)
    x3 = _out_proj(o1, proj1, (3 * dw) // mw, mem_kv, l1_w_out[:dw].astype(BF16),
                   l1_w_out[dw:].astype(BF16), x2, _row(l1_ln1_g), _row(l1_ln1_b), seq, mem_len)
    x4 = _mlp(x3, l1_w_ff1.astype(BF16), l1_w_ff2.astype(BF16), _row(l1_ln2_g), _row(l1_ln2_b))
    return x4.reshape(batch, seq, d)
```

```python
import functools
import math

import jax
import jax.numpy as jnp
from jax import lax
from jax.experimental import pallas as pl
from jax.experimental.pallas import tpu as pltpu

F32 = jnp.float32
BF16 = jnp.bfloat16

DEPTH = 2
GDN_HEADS = 8
GDN_HEAD_DIM = 128
CONV_WIDTH = 4
GDN_CHUNK = 64
DIFF_HEADS = 8
DIFF_HEAD_DIM = 64
MEM_HEADS = 4
MEM_HEAD_DIM = 128
ALPHA = (2.0 * DEPTH) ** 0.25
LN_EPS = 1e-5
RMS_EPS = 1e-6

LANES = 128
SUBLANES = 8
VMEM_LIMIT_BYTES = 56 * 1024 * 1024

NEG_BIG = -1e30


def _params(n_axes):
    return pltpu.CompilerParams(
        dimension_semantics=("arbitrary",) * n_axes,
        vmem_limit_bytes=VMEM_LIMIT_BYTES,
    )


def _resident(shape):
    nd = len(shape)
    return pl.BlockSpec(shape, lambda *_: (0,) * nd, pipeline_mode=pl.Buffered(1))


def _dot(a, b):
    return jnp.dot(a, b, preferred_element_type=F32)


def _dot_nt(a, b):
    return lax.dot_general(a, b, (((1,), (1,)), ((), ())), preferred_element_type=F32)


def _dot_tn(a, b):
    return lax.dot_general(a, b, (((0,), (0,)), ((), ())), preferred_element_type=F32)


def _layer_norm(y, g, b):
    mu = jnp.mean(y, axis=-1, keepdims=True)
    yc = y - mu
    var = jnp.mean(yc * yc, axis=-1, keepdims=True)
    return yc * lax.rsqrt(var + LN_EPS) * g + b


def _silu(x):
    return x * jax.nn.sigmoid(x)


def _softplus(x):
    return jnp.maximum(x, 0.0) + jnp.log1p(jnp.exp(-jnp.abs(x)))


def _mem_kv_kernel(mem_ref, g_ref, b_ref, w_ref, o_ref):
    y = _layer_norm(mem_ref[...], g_ref[...], b_ref[...])
    o_ref[...] = _dot(y.astype(BF16), w_ref[...]).astype(o_ref.dtype)


def _mem_kv(mem2d, g, b, w, rows):
    n, d = mem2d.shape
    nout = w.shape[1]
    return pl.pallas_call(
        _mem_kv_kernel,
        grid=(n // rows,),
        in_specs=[
            pl.BlockSpec((rows, d), lambda i: (i, 0)),
            _resident((1, d)),
            _resident((1, d)),
            _resident((d, nout)),
        ],
        out_specs=pl.BlockSpec((rows, nout), lambda i: (i, 0)),
        out_shape=jax.ShapeDtypeStruct((n, nout), BF16),
        compiler_params=_params(1),
        name="mem_kv",
    )(mem2d, g, b, w)


def _in_proj_kernel(x_ref, w_ref, o_ref, *, col_chunk, lead_cols, lead_scale):
    xb = x_ref[...].astype(BF16)
    nout = o_ref.shape[1]
    for c0 in range(0, nout, col_chunk):
        c1 = min(c0 + col_chunk, nout)
        acc = _dot(xb, w_ref[:, c0:c1])
        if c1 <= lead_cols:
            acc = acc * lead_scale
        o_ref[:, c0:c1] = acc.astype(o_ref.dtype)


def _in_proj(x2d, w, out_dtype, tm=512, col_chunk=512, lead_cols=0, lead_scale=1.0):
    n, d = x2d.shape
    nout = w.shape[1]
    assert lead_cols % col_chunk == 0
    return pl.pallas_call(
        functools.partial(_in_proj_kernel, col_chunk=col_chunk, lead_cols=lead_cols,
                          lead_scale=lead_scale),
        grid=(n // tm,),
        in_specs=[pl.BlockSpec((tm, d), lambda i: (i, 0)), _resident((d, nout))],
        out_specs=pl.BlockSpec((tm, nout), lambda i: (i, 0)),
        out_shape=jax.ShapeDtypeStruct((n, nout), out_dtype),
        compiler_params=_params(1),
        name="in_proj",
    )(x2d, w)


def _gdn_kernel(qkv_ref, z_ref, ab_ref, cw_ref, hp_ref, gw_ref, o_ref, s_ref, tail_ref,
                *, heads, hd, chunk):
    @pl.when(pl.program_id(1) == 0)
    def _():
        s_ref[...] = jnp.zeros_like(s_ref)
        tail_ref[...] = jnp.zeros_like(tail_ref)

    rows = qkv_ref.shape[0]
    width = heads * hd
    cw = cw_ref[...]
    neg_a = -jnp.exp(hp_ref[0:1, :])
    dt_bias = hp_ref[1:2, :]
    gate_w = gw_ref[...]
    scale = hd ** -0.5

    ri = lax.broadcasted_iota(jnp.int32, (chunk, chunk), 0)
    ci = lax.broadcasted_iota(jnp.int32, (chunk, chunk), 1)
    causal = ri >= ci
    strict = ri > ci
    eye = ri == ci
    tri = causal.astype(F32)
    eye_f = eye.astype(F32)

    def chunk_body(c, carry):
        r0 = pl.multiple_of(c * chunk, chunk)
        xc = qkv_ref[pl.ds(r0, chunk), :]
        ext = jnp.concatenate([tail_ref[...], xc], axis=0)
        conv = cw[CONV_WIDTH - 1:CONV_WIDTH] * xc
        for j in range(CONV_WIDTH - 1):
            off = SUBLANES - (CONV_WIDTH - 1) + j
            conv = conv + cw[j:j + 1] * ext[off:off + chunk]
        tail_ref[...] = xc[chunk - SUBLANES:chunk]
        act = _silu(conv)

        ab = ab_ref[pl.ds(r0, chunk), :]
        g_all = neg_a * _softplus(ab + dt_bias)
        beta_all = jax.nn.sigmoid(ab)
        gc_all = jnp.dot(tri, g_all, precision=lax.Precision.HIGHEST,
                         preferred_element_type=F32)
        zc = z_ref[pl.ds(r0, chunk), :]

        hs = range(heads)
        kn, qs, vb, kb, eg, gcol, decay = [], [], [], [], [], [], []
        for h in hs:
            qh = act[:, h * hd:(h + 1) * hd]
            kh = act[:, width + h * hd:width + (h + 1) * hd]
            vh = act[:, 2 * width + h * hd:2 * width + (h + 1) * hd]
            qn = qh * lax.rsqrt(jnp.sum(qh * qh, axis=-1, keepdims=True) + RMS_EPS)
            kn.append(kh * lax.rsqrt(jnp.sum(kh * kh, axis=-1, keepdims=True) + RMS_EPS))
            gc = gc_all[:, h:h + 1]
            bcol = beta_all[:, heads + h:heads + h + 1]
            g_i = jnp.broadcast_to(gc, (chunk, chunk))
            g_j = jnp.sum(jnp.where(eye, g_i, 0.0), axis=0, keepdims=True)
            decay.append(jnp.where(causal, jnp.exp(jnp.where(causal, g_i - g_j, 0.0)), 0.0))
            gcol.append(gc)
            eg.append(jnp.exp(gc))
            kb.append(kn[h] * bcol)
            vb.append(vh * bcol)
            qs.append(qn * scale)

        r1 = [_dot_nt(jnp.concatenate([kb[h], qs[h]], axis=0).astype(BF16), kn[h].astype(BF16))
              for h in hs]
        low = [jnp.where(strict, r1[h][:chunk] * decay[h], 0.0) for h in hs]
        attn = [r1[h][chunk:] * decay[h] for h in hs]

        x = [eye_f - low[h] for h in hs]
        m = [_dot(low[h].astype(BF16), low[h].astype(BF16)) for h in hs]
        for _ in range(4):
            r2 = [_dot(jnp.concatenate([x[h], m[h]], axis=0).astype(BF16), m[h].astype(BF16))
                  for h in hs]
            x = [x[h] + r2[h][:chunk] for h in hs]
            m = [r2[h][chunk:] for h in hs]
        x = [x[h] + _dot(x[h].astype(BF16), m[h].astype(BF16)) for h in hs]

        uw = [_dot(x[h].astype(BF16),
                   jnp.concatenate([vb[h], kb[h] * eg[h]], axis=1).astype(BF16)) for h in hs]
        state = [s_ref[h] for h in hs]
        r3 = [_dot(jnp.concatenate([uw[h][:, hd:], qs[h] * eg[h]], axis=0).astype(BF16),
                   state[h].astype(BF16)) for h in hs]
        vnb = [(uw[h][:, :hd] - r3[h][:chunk]).astype(BF16) for h in hs]
        o = [r3[h][chunk:] + _dot(attn[h].astype(BF16), vnb[h]) for h in hs]
        outs = []
        for h in hs:
            g_last = gcol[h][chunk - 1:chunk, :]
            k_dec = kn[h] * jnp.exp(g_last - gcol[h])
            s_ref[h] = state[h] * jnp.exp(g_last) + _dot_tn(k_dec.astype(BF16), vnb[h])
            on = o[h] * lax.rsqrt(jnp.mean(o[h] * o[h], axis=-1, keepdims=True) + RMS_EPS) * gate_w
            outs.append(on * _silu(zc[:, h * hd:(h + 1) * hd]))

        o_ref[pl.ds(r0, chunk), :] = jnp.concatenate(outs, axis=-1).astype(o_ref.dtype)
        return carry

    lax.fori_loop(0, rows // chunk, chunk_body, 0)


def _gdn(proj, conv_w, head_params, gate_w, batch, seq, rows=512):
    heads, hd = GDN_HEADS, GDN_HEAD_DIM
    width = heads * hd
    n = batch * seq
    nr = seq // rows
    row_map = lambda b, i: b * nr + i
    ab_block = (4 * width + MEM_HEADS * MEM_HEAD_DIM) // LANES
    return pl.pallas_call(
        functools.partial(_gdn_kernel, heads=heads, hd=hd, chunk=GDN_CHUNK),
        grid=(batch, nr),
        in_specs=[
            pl.BlockSpec((rows, 3 * width), lambda b, i: (row_map(b, i), 0)),
            pl.BlockSpec((rows, width), lambda b, i: (row_map(b, i), 3)),
            pl.BlockSpec((rows, LANES), lambda b, i: (row_map(b, i), ab_block)),
            _resident((CONV_WIDTH, 3 * width)),
            _resident((2, LANES)),
            _resident((1, hd)),
        ],
        out_specs=pl.BlockSpec((rows, width), lambda b, i: (row_map(b, i), 0)),
        out_shape=jax.ShapeDtypeStruct((n, width), BF16),
        scratch_shapes=[
            pltpu.VMEM((heads, hd, hd), F32),
            pltpu.VMEM((SUBLANES, 3 * width), F32),
        ],
        compiler_params=_params(2),
        name="gdn",
    )(proj, proj, proj, conv_w, head_params, gate_w)


def _diff_attn_kernel(q_ref, k_ref, vt_ref, lp_ref, sw_ref, o_ref, *, tq, tk, lam_init):
    i = pl.program_id(2)
    d = DIFF_HEAD_DIM
    q = q_ref[...]
    lane = lax.broadcasted_iota(jnp.int32, q.shape, 1)
    zero = jnp.zeros_like(q)
    qs = jnp.concatenate([jnp.where(lane < d, q, zero), jnp.where(lane >= d, q, zero)], axis=0)

    def step(j, carry, masked):
        m, l, acc = carry
        r0 = pl.multiple_of(j * tk, tk)
        s = _dot_nt(k_ref[pl.ds(r0, tk), :], qs)
        if masked:
            key = r0 + lax.broadcasted_iota(jnp.int32, s.shape, 0)
            qi = lax.broadcasted_iota(jnp.int32, s.shape, 1)
            qpos = i * tq + jnp.where(qi >= tq, qi - tq, qi)
            s = jnp.where(key <= qpos, s, NEG_BIG)
        m_new = jnp.maximum(m, jnp.max(s, axis=0, keepdims=True))
        a = jnp.exp2(m - m_new)
        p = jnp.exp2(s - m_new)
        l = a * l + jnp.sum(p, axis=0, keepdims=True)
        acc = a * acc + _dot(vt_ref[:, pl.ds(r0, tk)], p.astype(BF16))
        return m_new, l, acc

    init = (jnp.full((1, 2 * tq), NEG_BIG, F32), jnp.zeros((1, 2 * tq), F32),
            jnp.zeros((2 * d, 2 * tq), F32))
    n_full = (i * tq) // tk
    n_end = ((i + 1) * tq + tk - 1) // tk
    carry = lax.fori_loop(0, n_full, lambda j, c: step(j, c, False), init)
    _, l, acc = lax.fori_loop(n_full, n_end, lambda j, c: step(j, c, True), carry)

    lp = lp_ref[...]
    lam = (jnp.exp(jnp.sum(lp[0:1] * lp[1:2], axis=-1, keepdims=True))
           - jnp.exp(jnp.sum(lp[2:3] * lp[3:4], axis=-1, keepdims=True)) + lam_init)
    o = acc[:, :tq] / l[:, :tq] - lam * (acc[:, tq:] / l[:, tq:])
    on = o * lax.rsqrt(jnp.mean(o * o, axis=0, keepdims=True) + RMS_EPS) * sw_ref[...]
    o_ref[...] = (on * (1.0 - lam_init)).T.astype(o_ref.dtype)


def _diff_attn(proj, v_t, lam_params, subln_w, batch, seq, lam_init, tq=256, tk=512):
    heads = DIFF_HEADS
    hw = 2 * DIFF_HEAD_DIM
    n = batch * seq
    nq = seq // tq
    return pl.pallas_call(
        functools.partial(_diff_attn_kernel, tq=tq, tk=tk, lam_init=lam_init),
        grid=(batch, heads, nq),
        in_specs=[
            pl.BlockSpec((tq, hw), lambda b, h, i: (b * nq + i, h)),
            pl.BlockSpec((seq, hw), lambda b, h, i: (b, heads + h)),
            pl.BlockSpec((hw, seq), lambda b, h, i: (b * heads + h, 0)),
            _resident((4, hw)),
            _resident((hw, 1)),
        ],
        out_specs=pl.BlockSpec((tq, hw), lambda b, h, i: (b * nq + i, h)),
        out_shape=jax.ShapeDtypeStruct((n, heads * hw), BF16),
        compiler_params=_params(3),
        name="diff_attn",
    )(proj, proj, v_t, lam_params, subln_w)


def _out_proj_kernel(o_ref, mq_ref, kv_ref, wo_ref, wm_ref, x_ref, g_ref, b_ref, out_ref):
    mq = mq_ref[...].astype(BF16)
    kv = kv_ref[...]
    mw = MEM_HEADS * MEM_HEAD_DIM
    parts = []
    for h in range(MEM_HEADS):
        lo, hi = h * MEM_HEAD_DIM, (h + 1) * MEM_HEAD_DIM
        s = _dot_nt(mq[:, lo:hi], kv[:, lo:hi]) * (MEM_HEAD_DIM ** -0.5)
        p = jnp.exp(s - jnp.max(s, axis=-1, keepdims=True))
        pv = _dot(p.astype(BF16), kv[:, mw + lo:mw + hi])
        parts.append(pv / jnp.sum(p, axis=-1, keepdims=True))
    m = jnp.concatenate(parts, axis=-1).astype(BF16)
    mix = _dot(o_ref[...], wo_ref[...]) + _dot(m, wm_ref[...])
    y = ALPHA * x_ref[...] + mix
    out_ref[...] = _layer_norm(y, g_ref[...], b_ref[...])


def _out_proj(o, proj, mq_block, mem_kv, w_o, w_m, x2d, g, b, seq, mem_len, tm=512):
    n, d = x2d.shape
    mw = MEM_HEADS * MEM_HEAD_DIM
    per_batch = seq // tm
    return pl.pallas_call(
        _out_proj_kernel,
        grid=(n // tm,),
        in_specs=[
            pl.BlockSpec((tm, o.shape[1]), lambda i: (i, 0)),
            pl.BlockSpec((tm, mw), lambda i: (i, mq_block)),
            pl.BlockSpec((mem_len, 2 * mw), lambda i: (i // per_batch, 0)),
            _resident(w_o.shape),
            _resident(w_m.shape),
            pl.BlockSpec((tm, d), lambda i: (i, 0)),
            _resident((1, d)),
            _resident((1, d)),
        ],
        out_specs=pl.BlockSpec((tm, d), lambda i: (i, 0)),
        out_shape=jax.ShapeDtypeStruct((n, d), F32),
        compiler_params=_params(1),
        name="out_proj",
    )(o, proj, mem_kv, w_o, w_m, x2d, g, b)


def _mlp_kernel(x_ref, w1_ref, w2_ref, g_ref, b_ref, out_ref, *, ff_chunk):
    x = x_ref[...]
    xb = x.astype(BF16)
    d_ff = w1_ref.shape[1]
    acc = jnp.zeros(x.shape, F32)
    for c0 in range(0, d_ff, ff_chunk):
        h = jnp.maximum(_dot(xb, w1_ref[:, c0:c0 + ff_chunk]), 0.0)
        acc = acc + _dot((h * h).astype(BF16), w2_ref[c0:c0 + ff_chunk, :])
    out_ref[...] = _layer_norm(ALPHA * x + acc, g_ref[...], b_ref[...])


def _mlp(x2d, w1, w2, g, b, tm=512, ff_chunk=512):
    n, d = x2d.shape
    return pl.pallas_call(
        functools.partial(_mlp_kernel, ff_chunk=ff_chunk),
        grid=(n // tm,),
        in_specs=[
            pl.BlockSpec((tm, d), lambda i: (i, 0)),
            _resident(w1.shape),
            _resident(w2.shape),
            _resident((1, d)),
            _resident((1, d)),
        ],
        out_specs=pl.BlockSpec((tm, d), lambda i: (i, 0)),
        out_shape=jax.ShapeDtypeStruct((n, d), F32),
        compiler_params=_params(1),
        name="mlp",
    )(x2d, w1, w2, g, b)


def _row(v):
    return v.reshape(1, -1).astype(F32)


def _pad_lanes(v, lanes=LANES):
    return jnp.pad(v.astype(F32), (0, lanes - v.shape[0]))


def kernel(x, mem, mem_ln_g, mem_ln_b, w_mem_kv, l0_w_in, l0_conv_w, l0_a_log, l0_dt_bias, l0_gate_norm_w, l0_w_out, l0_ln1_g, l0_ln1_b, l0_w_ff1, l0_w_ff2, l0_ln2_g, l0_ln2_b, l1_w_in, l1_lambda_q1, l1_lambda_k1, l1_lambda_q2, l1_lambda_k2, l1_subln_w, l1_w_out, l1_ln1_g, l1_ln1_b, l1_w_ff1, l1_w_ff2, l1_ln2_g, l1_ln2_b):
    batch, seq, d = x.shape
    mem_len = mem.shape[1]
    n = batch * seq
    gw = GDN_HEADS * GDN_HEAD_DIM
    dw = 2 * DIFF_HEADS * DIFF_HEAD_DIM
    mw = MEM_HEADS * MEM_HEAD_DIM

    x2d = x.reshape(n, d)
    mem_kv = _mem_kv(mem.reshape(batch * mem_len, d), _row(mem_ln_g), _row(mem_ln_b),
                     w_mem_kv.astype(BF16), rows=mem_len)

    ab_cols = l0_w_in[:, 4 * gw:4 * gw + 2 * GDN_HEADS]
    w_in0 = jnp.concatenate(
        [l0_w_in[:, :4 * gw], l0_w_in[:, 4 * gw + 2 * GDN_HEADS:],
         jnp.pad(ab_cols, ((0, 0), (0, LANES - 2 * GDN_HEADS)))], axis=1).astype(BF16)
    proj0 = _in_proj(x2d, w_in0, F32)
    head_params = jnp.stack([_pad_lanes(l0_a_log), _pad_lanes(l0_dt_bias)])
    o0 = _gdn(proj0, l0_conv_w.astype(F32), head_params, _row(l0_gate_norm_w), batch, seq)
    x1 = _out_proj(o0, proj0, (4 * gw) // mw, mem_kv, l0_w_out[:gw].astype(BF16),
                   l0_w_out[gw:].astype(BF16), x2d, _row(l0_ln1_g), _row(l0_ln1_b), seq, mem_len)
    x2 = _mlp(x1, l0_w_ff1.astype(BF16), l0_w_ff2.astype(BF16), _row(l0_ln2_g), _row(l0_ln2_b))

    lam_init = 0.8 - 0.6 * math.exp(-0.3 * 1)
    q_scale = DIFF_HEAD_DIM ** -0.5 * math.log2(math.e)
    proj1 = _in_proj(x2, l1_w_in.astype(BF16), BF16, lead_cols=dw, lead_scale=q_scale)
    lam_params = jnp.stack([_pad_lanes(v) for v in
                            (l1_lambda_q1, l1_lambda_k1, l1_lambda_q2, l1_lambda_k2)])
    v_t = proj1[:, 2 * dw:3 * dw].reshape(batch, seq, dw).transpose(0, 2, 1).reshape(batch * dw, seq)
    o1 = _diff_attn(proj1, v_t, lam_params, l1_subln_w.reshape(-1, 1).astype(F32), batch, seq,
                    lam_init)
    x3 = _out_proj(o1, proj1, (3 * dw) // mw, mem_kv, l1_w_out[:dw].astype(BF16),
                   l1_w_out[dw:].astype(BF16), x2, _row(l1_ln1_g), _row(l1_ln1_b), seq, mem_len)
    x4 = _mlp(x3, l1_w_ff1.astype(BF16), l1_w_ff2.astype(BF16), _row(l1_ln2_g), _row(l1_ln2_b))
    return x4.reshape(batch, seq, d)
```

```python
import functools
import math

import jax
import jax.numpy as jnp
from jax import lax
from jax.experimental import pallas as pl
from jax.experimental.pallas import tpu as pltpu

F32 = jnp.float32
BF16 = jnp.bfloat16

DEPTH = 2
GDN_HEADS = 8
GDN_HEAD_DIM = 128
CONV_WIDTH = 4
GDN_CHUNK = 64
DIFF_HEADS = 8
DIFF_HEAD_DIM = 64
MEM_HEADS = 4
MEM_HEAD_DIM = 128
ALPHA = (2.0 * DEPTH) ** 0.25
LN_EPS = 1e-5
RMS_EPS = 1e-6

LANES = 128
SUBLANES = 8
BF16_ROWS = 16
VMEM_LIMIT_BYTES = 56 * 1024 * 1024

NEG_BIG = -1e30


def _params(n_axes):
    return pltpu.CompilerParams(
        dimension_semantics=("arbitrary",) * n_axes,
        vmem_limit_bytes=VMEM_LIMIT_BYTES,
    )


def _resident(shape):
    nd = len(shape)
    return pl.BlockSpec(shape, lambda *_: (0,) * nd, pipeline_mode=pl.Buffered(1))


def _dot(a, b):
    return jnp.dot(a, b, preferred_element_type=F32)


def _dot_nt(a, b):
    return lax.dot_general(a, b, (((1,), (1,)), ((), ())), preferred_element_type=F32)


def _dot_tn(a, b):
    return lax.dot_general(a, b, (((0,), (0,)), ((), ())), preferred_element_type=F32)


def _layer_norm(y, g, b):
    mu = jnp.mean(y, axis=-1, keepdims=True)
    yc = y - mu
    var = jnp.mean(yc * yc, axis=-1, keepdims=True)
    return yc * lax.rsqrt(var + LN_EPS) * g + b


def _silu(x):
    return x * jax.nn.sigmoid(x)


def _softplus(x):
    return jnp.maximum(x, 0.0) + jnp.log1p(jnp.exp(-jnp.abs(x)))


def _mem_kv_kernel(mem_ref, g_ref, b_ref, w_ref, o_ref):
    y = _layer_norm(mem_ref[...], g_ref[...], b_ref[...])
    o_ref[...] = _dot(y.astype(BF16), w_ref[...]).astype(o_ref.dtype)


def _mem_kv(mem2d, g, b, w, rows):
    n, d = mem2d.shape
    nout = w.shape[1]
    return pl.pallas_call(
        _mem_kv_kernel,
        grid=(n // rows,),
        in_specs=[
            pl.BlockSpec((rows, d), lambda i: (i, 0)),
            _resident((1, d)),
            _resident((1, d)),
            _resident((d, nout)),
        ],
        out_specs=pl.BlockSpec((rows, nout), lambda i: (i, 0)),
        out_shape=jax.ShapeDtypeStruct((n, nout), BF16),
        compiler_params=_params(1),
        name="mem_kv",
    )(mem2d, g, b, w)


def _in_proj_kernel(x_ref, w_ref, o_ref, *, col_chunk, lead_cols, lead_scale):
    xb = x_ref[...].astype(BF16)
    nout = o_ref.shape[1]
    for c0 in range(0, nout, col_chunk):
        c1 = min(c0 + col_chunk, nout)
        acc = _dot(xb, w_ref[:, c0:c1])
        if c1 <= lead_cols:
            acc = acc * lead_scale
        o_ref[:, c0:c1] = acc.astype(o_ref.dtype)


def _in_proj_conv_kernel(x_ref, w_ref, cw_ref, o_ref, tail_ref, *, col_chunk, conv_cols,
                         tiles_per_seq):
    @pl.when(pl.program_id(0) == 0)
    def _():
        tail_ref[...] = jnp.zeros_like(tail_ref)

    xb = x_ref[...].astype(BF16)
    tm = x_ref.shape[0]
    nout = o_ref.shape[1]
    seq_start = (pl.program_id(0) % tiles_per_seq) == 0
    for c0 in range(0, nout, col_chunk):
        c1 = min(c0 + col_chunk, nout)
        acc = _dot(xb, w_ref[:, c0:c1])
        if c1 <= conv_cols:
            tail = jnp.where(seq_start, 0.0, tail_ref[:, c0:c1])
            ext = jnp.concatenate([tail, acc], axis=0)
            cw = cw_ref[:, c0:c1]
            conv = cw[CONV_WIDTH - 1:CONV_WIDTH] * acc
            for j in range(CONV_WIDTH - 1):
                off = SUBLANES - (CONV_WIDTH - 1) + j
                conv = conv + cw[j:j + 1] * ext[off:off + tm]
            tail_ref[:, c0:c1] = acc[tm - SUBLANES:tm]
            acc = _silu(conv)
        o_ref[:, c0:c1] = acc.astype(o_ref.dtype)


def _in_proj_conv(x2d, w, conv_w, seq, tm=512, col_chunk=512):
    n, d = x2d.shape
    nout = w.shape[1]
    conv_cols = conv_w.shape[1]
    assert conv_cols % col_chunk == 0 and seq % tm == 0
    return pl.pallas_call(
        functools.partial(_in_proj_conv_kernel, col_chunk=col_chunk, conv_cols=conv_cols,
                          tiles_per_seq=seq // tm),
        grid=(n // tm,),
        in_specs=[pl.BlockSpec((tm, d), lambda i: (i, 0)), _resident((d, nout)),
                  _resident(conv_w.shape)],
        out_specs=pl.BlockSpec((tm, nout), lambda i: (i, 0)),
        out_shape=jax.ShapeDtypeStruct((n, nout), F32),
        scratch_shapes=[pltpu.VMEM((SUBLANES, conv_cols), F32)],
        compiler_params=_params(1),
        name="in_proj_conv",
    )(x2d, w, conv_w)


def _in_proj(x2d, w, out_dtype, tm=512, col_chunk=512, lead_cols=0, lead_scale=1.0):
    n, d = x2d.shape
    nout = w.shape[1]
    assert lead_cols % col_chunk == 0
    return pl.pallas_call(
        functools.partial(_in_proj_kernel, col_chunk=col_chunk, lead_cols=lead_cols,
                          lead_scale=lead_scale),
        grid=(n // tm,),
        in_specs=[pl.BlockSpec((tm, d), lambda i: (i, 0)), _resident((d, nout))],
        out_specs=pl.BlockSpec((tm, nout), lambda i: (i, 0)),
        out_shape=jax.ShapeDtypeStruct((n, nout), out_dtype),
        compiler_params=_params(1),
        name="in_proj",
    )(x2d, w)


def _gdn_kernel(qkv_ref, z_ref, ab_ref, hp_ref, gw_ref, o_ref, s_ref, *, heads, hd, chunk):
    @pl.when(pl.program_id(1) == 0)
    def _():
        s_ref[...] = jnp.zeros_like(s_ref)

    nb, rows = qkv_ref.shape[0], qkv_ref.shape[1]
    width = heads * hd
    neg_a = -jnp.exp(hp_ref[0:1, :])
    dt_bias = hp_ref[1:2, :]
    gate_w = gw_ref[...]
    scale = hd ** -0.5

    ri = lax.broadcasted_iota(jnp.int32, (chunk, chunk), 0)
    ci = lax.broadcasted_iota(jnp.int32, (chunk, chunk), 1)
    eye = ri == ci
    tri = (ri >= ci).astype(F32)
    strict_f = (ri > ci).astype(F32)
    eye_f = eye.astype(F32)

    def chunk_body(c, carry):
        r0 = pl.multiple_of(c * chunk, chunk)
        hs = range(nb * heads)
        kn, qs, vb, kb, eg, gcol, decay, zs = [], [], [], [], [], [], [], []
        for s in range(nb):
            act = qkv_ref[s, pl.ds(r0, chunk), :]
            ab = ab_ref[s, pl.ds(r0, chunk), :]
            g_all = neg_a * _softplus(ab + dt_bias)
            beta_all = jax.nn.sigmoid(ab)
            gc_all = jnp.dot(tri, g_all, precision=lax.Precision.HIGHEST,
                             preferred_element_type=F32)
            zc = z_ref[s, pl.ds(r0, chunk), :]

            for h in range(heads):
                qh = act[:, h * hd:(h + 1) * hd]
                kh = act[:, width + h * hd:width + (h + 1) * hd]
                vh = act[:, 2 * width + h * hd:2 * width + (h + 1) * hd]
                qn = qh * lax.rsqrt(jnp.sum(qh * qh, axis=-1, keepdims=True) + RMS_EPS)
                kn.append(kh * lax.rsqrt(jnp.sum(kh * kh, axis=-1, keepdims=True) + RMS_EPS))
                gc = gc_all[:, h:h + 1]
                bcol = beta_all[:, heads + h:heads + h + 1]
                g_i = jnp.broadcast_to(gc, (chunk, chunk))
                g_j = jnp.sum(jnp.where(eye, g_i, 0.0), axis=0, keepdims=True)
                decay.append(tri * jnp.exp(jnp.minimum(g_i - g_j, 0.0)))
                gcol.append(gc)
                eg.append(jnp.exp(gc))
                kb.append(kn[-1] * bcol)
                vb.append(vh * bcol)
                qs.append(qn * scale)
                zs.append(zc[:, h * hd:(h + 1) * hd])

        r1 = [_dot_nt(jnp.concatenate([kb[h], qs[h]], axis=0).astype(BF16), kn[h].astype(BF16))
              for h in hs]
        low = [r1[h][:chunk] * (decay[h] * strict_f) for h in hs]
        attn = [r1[h][chunk:] * decay[h] for h in hs]

        x = [eye_f - low[h] for h in hs]
        m = [_dot(low[h].astype(BF16), low[h].astype(BF16)) for h in hs]
        for _ in range(4):
            r2 = [_dot(jnp.concatenate([x[h], m[h]], axis=0).astype(BF16), m[h].astype(BF16))
                  for h in hs]
            x = [x[h] + r2[h][:chunk] for h in hs]
            m = [r2[h][chunk:] for h in hs]
        x = [x[h] + _dot(x[h].astype(BF16), m[h].astype(BF16)) for h in hs]

        uw = [_dot(x[h].astype(BF16),
                   jnp.concatenate([vb[h], kb[h] * eg[h]], axis=1).astype(BF16)) for h in hs]
        state = [s_ref[h] for h in hs]
        r3 = [_dot(jnp.concatenate([uw[h][:, hd:], qs[h] * eg[h]], axis=0).astype(BF16),
                   state[h].astype(BF16)) for h in hs]
        vnb = [(uw[h][:, :hd] - r3[h][:chunk]).astype(BF16) for h in hs]
        o = [r3[h][chunk:] + _dot(attn[h].astype(BF16), vnb[h]) for h in hs]
        outs = []
        for h in hs:
            g_last = gcol[h][chunk - 1:chunk, :]
            k_dec = kn[h] * jnp.exp(g_last - gcol[h])
            s_ref[h] = state[h] * jnp.exp(g_last) + _dot_tn(k_dec.astype(BF16), vnb[h])
            on = o[h] * lax.rsqrt(jnp.mean(o[h] * o[h], axis=-1, keepdims=True) + RMS_EPS) * gate_w
            outs.append(on * _silu(zs[h]))

        for s in range(nb):
            o_ref[s, pl.ds(r0, chunk), :] = jnp.concatenate(
                outs[s * heads:(s + 1) * heads], axis=-1).astype(o_ref.dtype)
        return carry

    lax.fori_loop(0, rows // chunk, chunk_body, 0)


def _gdn(proj, head_params, gate_w, rows=256, nb=2):
    heads, hd = GDN_HEADS, GDN_HEAD_DIM
    width = heads * hd
    batch, seq, _ = proj.shape
    ab_block = (4 * width + MEM_HEADS * MEM_HEAD_DIM) // LANES
    return pl.pallas_call(
        functools.partial(_gdn_kernel, heads=heads, hd=hd, chunk=GDN_CHUNK),
        grid=(batch // nb, seq // rows),
        in_specs=[
            pl.BlockSpec((nb, rows, 3 * width), lambda g, i: (g, i, 0)),
            pl.BlockSpec((nb, rows, width), lambda g, i: (g, i, 3)),
            pl.BlockSpec((nb, rows, LANES), lambda g, i: (g, i, ab_block)),
            _resident((2, LANES)),
            _resident((1, hd)),
        ],
        out_specs=pl.BlockSpec((nb, rows, width), lambda g, i: (g, i, 0)),
        out_shape=jax.ShapeDtypeStruct((batch, seq, width), BF16),
        scratch_shapes=[pltpu.VMEM((nb * heads, hd, hd), F32)],
        compiler_params=_params(2),
        name="gdn",
    )(proj, proj, proj, head_params, gate_w)


def _diff_attn_kernel(q_ref, k_ref, vt_ref, lp_ref, sw_ref, o_ref, s_scr, *, tq, tk, nh, lam_init):
    i = pl.program_id(2)
    d = DIFF_HEAD_DIM
    hw = 2 * d
    chains = range(nh)

    qs = []
    for c in chains:
        q = q_ref[:, c * hw:(c + 1) * hw]
        lane = lax.broadcasted_iota(jnp.int32, q.shape, 1)
        zero = jnp.zeros_like(q)
        qs.append(jnp.concatenate([jnp.where(lane < d, q, zero), jnp.where(lane >= d, q, zero)],
                                  axis=0))

    ones_rows = jnp.ones((BF16_ROWS, tk), BF16)

    def score_stage(j, buf):
        r0 = pl.multiple_of(j * tk, tk)
        block_max = []
        for c in chains:
            s = _dot_nt(k_ref[pl.ds(r0, tk), c * hw:(c + 1) * hw], qs[c])
            s_scr[buf, c] = s
            block_max.append(jnp.max(s, axis=0, keepdims=True))
        return tuple(block_max)

    def softmax_stage(j, buf, block_max, states, masked):
        r0 = pl.multiple_of(j * tk, tk)
        out = []
        for c in chains:
            m, l, acc = states[c]
            s = s_scr[buf, c]
            if masked:
                key = r0 + lax.broadcasted_iota(jnp.int32, s.shape, 0)
                qi = lax.broadcasted_iota(jnp.int32, s.shape, 1)
                qpos = i * tq + jnp.where(qi >= tq, qi - tq, qi)
                s = jnp.where(key <= qpos, s, NEG_BIG)
                bm = jnp.max(s, axis=0, keepdims=True)
            else:
                bm = block_max[c]
            m_new = jnp.maximum(m, bm)
            a = jnp.exp2(m - m_new)
            p = jnp.exp2(s - m_new).astype(BF16)
            vt1 = jnp.concatenate([vt_ref[c * hw:(c + 1) * hw, pl.ds(r0, tk)], ones_rows], axis=0)
            pv = _dot(vt1, p)
            out.append((m_new, a * l + pv[hw:hw + 1], a * acc + pv[:hw]))
        return tuple(out)

    def pair_body(t, carry):
        max_a, states = carry
        j = 2 * t
        max_b = score_stage(j + 1, 1)
        states = softmax_stage(j, 0, max_a, states, False)
        max_a = score_stage(j + 2, 0)
        states = softmax_stage(j + 1, 1, max_b, states, False)
        return max_a, states

    init = tuple((jnp.full((1, 2 * tq), NEG_BIG, F32), jnp.zeros((1, 2 * tq), F32),
                  jnp.zeros((hw, 2 * tq), F32)) for _ in chains)
    _, states = lax.fori_loop(0, i, pair_body, (score_stage(0, 0), init))
    score_stage(2 * i + 1, 1)
    states = softmax_stage(2 * i, 0, None, states, True)
    states = softmax_stage(2 * i + 1, 1, None, states, True)

    lp = lp_ref[...]
    lam = (jnp.exp(jnp.sum(lp[0:1] * lp[1:2], axis=-1, keepdims=True))
           - jnp.exp(jnp.sum(lp[2:3] * lp[3:4], axis=-1, keepdims=True)) + lam_init)
    for c in chains:
        _, l, acc = states[c]
        o = acc[:, :tq] / l[:, :tq] - lam * (acc[:, tq:] / l[:, tq:])
        on = o * lax.rsqrt(jnp.mean(o * o, axis=0, keepdims=True) + RMS_EPS) * sw_ref[...]
        o_ref[:, c * hw:(c + 1) * hw] = (on * (1.0 - lam_init)).T.astype(o_ref.dtype)


def _diff_attn(proj, v_t, lam_params, subln_w, batch, seq, lam_init, tq=512, nh=2):
    heads = DIFF_HEADS
    hw = 2 * DIFF_HEAD_DIM
    tk = tq // 2
    n = batch * seq
    nq = seq // tq
    groups = heads // nh
    return pl.pallas_call(
        functools.partial(_diff_attn_kernel, tq=tq, tk=tk, nh=nh, lam_init=lam_init),
        grid=(batch, groups, nq),
        in_specs=[
            pl.BlockSpec((tq, nh * hw), lambda b, g, i: (b * nq + i, g)),
            pl.BlockSpec((seq, nh * hw), lambda b, g, i: (b, groups + g)),
            pl.BlockSpec((nh * hw, seq), lambda b, g, i: (b * groups + g, 0)),
            _resident((4, hw)),
            _resident((hw, 1)),
        ],
        out_specs=pl.BlockSpec((tq, nh * hw), lambda b, g, i: (b * nq + i, g)),
        out_shape=jax.ShapeDtypeStruct((n, heads * hw), BF16),
        scratch_shapes=[pltpu.VMEM((2, nh, tk, 2 * tq), F32)],
        compiler_params=_params(3),
        name="diff_attn",
    )(proj, proj, v_t, lam_params, subln_w)


def _out_proj_kernel(o_ref, mq_ref, kv_ref, wo_ref, wm_ref, x_ref, g_ref, b_ref, out_ref):
    mq = mq_ref[...].astype(BF16)
    kv = kv_ref[...]
    mw = MEM_HEADS * MEM_HEAD_DIM
    parts = []
    for h in range(MEM_HEADS):
        lo, hi = h * MEM_HEAD_DIM, (h + 1) * MEM_HEAD_DIM
        s = _dot_nt(mq[:, lo:hi], kv[:, lo:hi]) * (MEM_HEAD_DIM ** -0.5)
        p = jnp.exp(s - jnp.max(s, axis=-1, keepdims=True))
        pv = _dot(p.astype(BF16), kv[:, mw + lo:mw + hi])
        parts.append(pv / jnp.sum(p, axis=-1, keepdims=True))
    m = jnp.concatenate(parts, axis=-1).astype(BF16)
    mix = _dot(o_ref[...], wo_ref[...]) + _dot(m, wm_ref[...])
    y = ALPHA * x_ref[...] + mix
    out_ref[...] = _layer_norm(y, g_ref[...], b_ref[...])


def _out_proj(o, proj, mq_block, mem_kv, w_o, w_m, x2d, g, b, seq, mem_len, tm=512):
    n, d = x2d.shape
    mw = MEM_HEADS * MEM_HEAD_DIM
    per_batch = seq // tm
    return pl.pallas_call(
        _out_proj_kernel,
        grid=(n // tm,),
        in_specs=[
            pl.BlockSpec((tm, o.shape[1]), lambda i: (i, 0)),
            pl.BlockSpec((tm, mw), lambda i: (i, mq_block)),
            pl.BlockSpec((mem_len, 2 * mw), lambda i: (i // per_batch, 0)),
            _resident(w_o.shape),
            _resident(w_m.shape),
            pl.BlockSpec((tm, d), lambda i: (i, 0)),
            _resident((1, d)),
            _resident((1, d)),
        ],
        out_specs=pl.BlockSpec((tm, d), lambda i: (i, 0)),
        out_shape=jax.ShapeDtypeStruct((n, d), F32),
        compiler_params=_params(1),
        name="out_proj",
    )(o, proj, mem_kv, w_o, w_m, x2d, g, b)


def _mlp_kernel(x_ref, w1_ref, w2_ref, g_ref, b_ref, out_ref, *, ff_chunk):
    x = x_ref[...]
    xb = x.astype(BF16)
    d_ff = w1_ref.shape[1]
    acc = jnp.zeros(x.shape, F32)
    for c0 in range(0, d_ff, ff_chunk):
        h = jnp.maximum(_dot(xb, w1_ref[:, c0:c0 + ff_chunk]), 0.0)
        acc = acc + _dot((h * h).astype(BF16), w2_ref[c0:c0 + ff_chunk, :])
    out_ref[...] = _layer_norm(ALPHA * x + acc, g_ref[...], b_ref[...])


def _mlp(x2d, w1, w2, g, b, tm=512, ff_chunk=512):
    n, d = x2d.shape
    return pl.pallas_call(
        functools.partial(_mlp_kernel, ff_chunk=ff_chunk),
        grid=(n // tm,),
        in_specs=[
            pl.BlockSpec((tm, d), lambda i: (i, 0)),
            _resident(w1.shape),
            _resident(w2.shape),
            _resident((1, d)),
            _resident((1, d)),
        ],
        out_specs=pl.BlockSpec((tm, d), lambda i: (i, 0)),
        out_shape=jax.ShapeDtypeStruct((n, d), F32),
        compiler_params=_params(1),
        name="mlp",
    )(x2d, w1, w2, g, b)


def _row(v):
    return v.reshape(1, -1).astype(F32)


def _pad_lanes(v, lanes=LANES):
    return jnp.pad(v.astype(F32), (0, lanes - v.shape[0]))


def kernel(x, mem, mem_ln_g, mem_ln_b, w_mem_kv, l0_w_in, l0_conv_w, l0_a_log, l0_dt_bias, l0_gate_norm_w, l0_w_out, l0_ln1_g, l0_ln1_b, l0_w_ff1, l0_w_ff2, l0_ln2_g, l0_ln2_b, l1_w_in, l1_lambda_q1, l1_lambda_k1, l1_lambda_q2, l1_lambda_k2, l1_subln_w, l1_w_out, l1_ln1_g, l1_ln1_b, l1_w_ff1, l1_w_ff2, l1_ln2_g, l1_ln2_b):
    batch, seq, d = x.shape
    mem_len = mem.shape[1]
    n = batch * seq
    gw = GDN_HEADS * GDN_HEAD_DIM
    dw = 2 * DIFF_HEADS * DIFF_HEAD_DIM
    mw = MEM_HEADS * MEM_HEAD_DIM

    x2d = x.reshape(n, d)
    mem_kv = _mem_kv(mem.reshape(batch * mem_len, d), _row(mem_ln_g), _row(mem_ln_b),
                     w_mem_kv.astype(BF16), rows=mem_len)

    ab_cols = l0_w_in[:, 4 * gw:4 * gw + 2 * GDN_HEADS]
    w_in0 = jnp.concatenate(
        [l0_w_in[:, :4 * gw], l0_w_in[:, 4 * gw + 2 * GDN_HEADS:],
         jnp.pad(ab_cols, ((0, 0), (0, LANES - 2 * GDN_HEADS)))], axis=1).astype(BF16)
    proj0 = _in_proj_conv(x2d, w_in0, l0_conv_w.astype(F32), seq)
    head_params = jnp.stack([_pad_lanes(l0_a_log), _pad_lanes(l0_dt_bias)])
    o0 = _gdn(proj0.reshape(batch, seq, -1), head_params, _row(l0_gate_norm_w)).reshape(n, gw)
    x1 = _out_proj(o0, proj0, (4 * gw) // mw, mem_kv, l0_w_out[:gw].astype(BF16),
                   l0_w_out[gw:].astype(BF16), x2d, _row(l0_ln1_g), _row(l0_ln1_b), seq, mem_len)
    x2 = _mlp(x1, l0_w_ff1.astype(BF16), l0_w_ff2.astype(BF16), _row(l0_ln2_g), _row(l0_ln2_b))

    lam_init = 0.8 - 0.6 * math.exp(-0.3 * 1)
    q_scale = DIFF_HEAD_DIM ** -0.5 * math.log2(math.e)
    proj1 = _in_proj(x2, l1_w_in.astype(BF16), BF16, lead_cols=dw, lead_scale=q_scale)
    lam_params = jnp.stack([_pad_lanes(v) for v in
                            (l1_lambda_q1, l1_lambda_k1, l1_lambda_q2, l1_lambda_k2)])
    v_t = proj1[:, 2 * dw:3 * dw].reshape(batch, seq, dw).transpose(0, 2, 1).reshape(batch * dw, seq)
    o1 = _diff_attn(proj1, v_t, lam_params, l1_subln_w.reshape(-1, 1).astype(F32), batch, seq,
                    lam_init)
    x3 = _out_proj(o1, proj1, (3 * dw) // mw, mem_kv, l1_w_out[:dw].astype(BF16),
                   l1_w_out[dw:].astype(BF16), x2, _row(l1_ln1_g), _row(l1_ln1_b), seq, mem_len)
    x4 = _mlp(x3, l1_w_ff1.astype(BF16), l1_w_ff2.astype(BF16), _row(l1_ln2_g), _row(l1_ln2_b))
    return x4.reshape(batch, seq, d)
```

```python
import functools
import math

import jax
import jax.numpy as jnp
from jax import lax
from jax.experimental import pallas as pl
from jax.experimental.pallas import tpu as pltpu

F32 = jnp.float32
BF16 = jnp.bfloat16

DEPTH = 2
GDN_HEADS = 8
GDN_HEAD_DIM = 128
CONV_WIDTH = 4
GDN_CHUNK = 64
DIFF_HEADS = 8
DIFF_HEAD_DIM = 64
MEM_HEADS = 4
MEM_HEAD_DIM = 128
ALPHA = (2.0 * DEPTH) ** 0.25
LN_EPS = 1e-5
RMS_EPS = 1e-6

LANES = 128
SUBLANES = 8
BF16_ROWS = 16
VMEM_LIMIT_BYTES = 56 * 1024 * 1024

NEG_BIG = -1e30


def _params(n_axes):
    return pltpu.CompilerParams(
        dimension_semantics=("arbitrary",) * n_axes,
        vmem_limit_bytes=VMEM_LIMIT_BYTES,
    )


def _resident(shape):
    nd = len(shape)
    return pl.BlockSpec(shape, lambda *_: (0,) * nd, pipeline_mode=pl.Buffered(1))


def _dot(a, b):
    return jnp.dot(a, b, preferred_element_type=F32)


def _dot_nt(a, b):
    return lax.dot_general(a, b, (((1,), (1,)), ((), ())), preferred_element_type=F32)


def _layer_norm(y, g, b):
    mu = jnp.mean(y, axis=-1, keepdims=True)
    yc = y - mu
    var = jnp.mean(yc * yc, axis=-1, keepdims=True)
    return yc * lax.rsqrt(var + LN_EPS) * g + b


def _silu(x):
    return x * jax.nn.sigmoid(x)


def _softplus(x):
    return jnp.maximum(x, 0.0) + jnp.log1p(jnp.exp(-jnp.abs(x)))


def _mem_kv_kernel(mem_ref, g_ref, b_ref, w_ref, o_ref):
    y = _layer_norm(mem_ref[...], g_ref[...], b_ref[...])
    o_ref[...] = _dot(y.astype(BF16), w_ref[...]).astype(o_ref.dtype)


def _mem_kv(mem2d, g, b, w, rows):
    n, d = mem2d.shape
    nout = w.shape[1]
    return pl.pallas_call(
        _mem_kv_kernel,
        grid=(n // rows,),
        in_specs=[
            pl.BlockSpec((rows, d), lambda i: (i, 0)),
            _resident((1, d)),
            _resident((1, d)),
            _resident((d, nout)),
        ],
        out_specs=pl.BlockSpec((rows, nout), lambda i: (i, 0)),
        out_shape=jax.ShapeDtypeStruct((n, nout), BF16),
        compiler_params=_params(1),
        name="mem_kv",
    )(mem2d, g, b, w)


def _in_proj_kernel(x_ref, w_ref, wt_ref, o_ref, ot_ref, *, col_chunk, lead_cols, lead_scale):
    xb = x_ref[...].astype(BF16)
    nout = o_ref.shape[1]
    for c0 in range(0, nout, col_chunk):
        c1 = min(c0 + col_chunk, nout)
        acc = _dot(xb, w_ref[:, c0:c1])
        if c1 <= lead_cols:
            acc = acc * lead_scale
        o_ref[:, c0:c1] = acc.astype(o_ref.dtype)
    for r0 in range(0, ot_ref.shape[1], col_chunk):
        ot_ref[0, r0:r0 + col_chunk, :] = _dot_nt(wt_ref[r0:r0 + col_chunk, :], xb).astype(ot_ref.dtype)


def _in_proj_conv_kernel(x_ref, w_ref, cw_ref, o_ref, tail_ref, *, col_chunk, conv_cols,
                         tiles_per_seq):
    @pl.when(pl.program_id(0) == 0)
    def _():
        tail_ref[...] = jnp.zeros_like(tail_ref)

    xb = x_ref[...].astype(BF16)
    tm = x_ref.shape[0]
    nout = o_ref.shape[1]
    seq_start = (pl.program_id(0) % tiles_per_seq) == 0
    for c0 in range(0, nout, col_chunk):
        c1 = min(c0 + col_chunk, nout)
        acc = _dot(xb, w_ref[:, c0:c1])
        if c1 <= conv_cols:
            tail = jnp.where(seq_start, 0.0, tail_ref[:, c0:c1])
            ext = jnp.concatenate([tail, acc], axis=0)
            cw = cw_ref[:, c0:c1]
            conv = cw[CONV_WIDTH - 1:CONV_WIDTH] * acc
            for j in range(CONV_WIDTH - 1):
                off = SUBLANES - (CONV_WIDTH - 1) + j
                conv = conv + cw[j:j + 1] * ext[off:off + tm]
            tail_ref[:, c0:c1] = acc[tm - SUBLANES:tm]
            acc = _silu(conv)
        o_ref[:, c0:c1] = acc.astype(o_ref.dtype)


def _in_proj_conv(x2d, w, conv_w, seq, tm=512, col_chunk=512):
    n, d = x2d.shape
    nout = w.shape[1]
    conv_cols = conv_w.shape[1]
    assert conv_cols % col_chunk == 0 and seq % tm == 0
    return pl.pallas_call(
        functools.partial(_in_proj_conv_kernel, col_chunk=col_chunk, conv_cols=conv_cols,
                          tiles_per_seq=seq // tm),
        grid=(n // tm,),
        in_specs=[pl.BlockSpec((tm, d), lambda i: (i, 0)), _resident((d, nout)),
                  _resident(conv_w.shape)],
        out_specs=pl.BlockSpec((tm, nout), lambda i: (i, 0)),
        out_shape=jax.ShapeDtypeStruct((n, nout), F32),
        scratch_shapes=[pltpu.VMEM((SUBLANES, conv_cols), F32)],
        compiler_params=_params(1),
        name="in_proj_conv",
    )(x2d, w, conv_w)


def _in_proj(x2d, w, w_t, seq, tm=512, col_chunk=512, lead_cols=0, lead_scale=1.0):
    n, d = x2d.shape
    nout = w.shape[1]
    nt = w_t.shape[0]
    per_seq = seq // tm
    assert lead_cols % col_chunk == 0 and nt % col_chunk == 0
    return pl.pallas_call(
        functools.partial(_in_proj_kernel, col_chunk=col_chunk, lead_cols=lead_cols,
                          lead_scale=lead_scale),
        grid=(n // tm,),
        in_specs=[pl.BlockSpec((tm, d), lambda i: (i, 0)), _resident((d, nout)),
                  _resident((nt, d))],
        out_specs=[pl.BlockSpec((tm, nout), lambda i: (i, 0)),
                   pl.BlockSpec((1, nt, tm), lambda i: (i // per_seq, 0, i % per_seq))],
        out_shape=[jax.ShapeDtypeStruct((n, nout), BF16),
                   jax.ShapeDtypeStruct((n // seq, nt, seq), BF16)],
        compiler_params=_params(1),
        name="in_proj",
    )(x2d, w, w_t)


def _gdn_kernel(qkv_ref, z_ref, ab_ref, hp_ref, gw_ref, o_ref, s_ref, *, heads, hd, chunk):
    @pl.when(pl.program_id(1) == 0)
    def _():
        s_ref[...] = jnp.zeros_like(s_ref)

    nb, rows = qkv_ref.shape[0], qkv_ref.shape[1]
    width = heads * hd
    neg_a = -jnp.exp(hp_ref[0:1, :])
    dt_bias = hp_ref[1:2, :]
    gate_w = gw_ref[...]
    scale = hd ** -0.5

    ri = lax.broadcasted_iota(jnp.int32, (chunk, chunk), 0)
    ci = lax.broadcasted_iota(jnp.int32, (chunk, chunk), 1)
    eye = ri == ci
    tri = (ri >= ci).astype(F32)
    strict_f = (ri > ci).astype(F32)
    eye_f = eye.astype(F32)

    def chunk_body(c, carry):
        r0 = pl.multiple_of(c * chunk, chunk)
        hs = range(nb * heads)
        kn, qs, vb, kb, eg, gcol, decay, zs, kdt = [], [], [], [], [], [], [], [], []
        for s in range(nb):
            act = qkv_ref[s, pl.ds(r0, chunk), :]
            ab = ab_ref[s, pl.ds(r0, chunk), :]
            g_all = neg_a * _softplus(ab + dt_bias)
            beta_all = jax.nn.sigmoid(ab)
            gc_all = jnp.dot(tri, g_all, precision=lax.Precision.HIGHEST,
                             preferred_element_type=F32)
            zc = z_ref[s, pl.ds(r0, chunk), :]

            for h in range(heads):
                qh = act[:, h * hd:(h + 1) * hd]
                kh = act[:, width + h * hd:width + (h + 1) * hd]
                vh = act[:, 2 * width + h * hd:2 * width + (h + 1) * hd]
                qn = qh * lax.rsqrt(jnp.sum(qh * qh, axis=-1, keepdims=True) + RMS_EPS)
                kn.append(kh * lax.rsqrt(jnp.sum(kh * kh, axis=-1, keepdims=True) + RMS_EPS))
                gc = gc_all[:, h:h + 1]
                bcol = beta_all[:, heads + h:heads + h + 1]
                g_i = jnp.broadcast_to(gc, (chunk, chunk))
                g_j = jnp.sum(jnp.where(eye, g_i, 0.0), axis=0, keepdims=True)
                decay.append(tri * jnp.exp(jnp.minimum(g_i - g_j, 0.0)))
                gcol.append(gc)
                eg.append(jnp.exp(gc))
                kb.append(kn[-1] * bcol)
                vb.append(vh * bcol)
                qs.append(qn * scale)
                zs.append(zc[:, h * hd:(h + 1) * hd])
                kdt.append((kn[-1] * jnp.exp(gc[chunk - 1:chunk, :] - gc)).T.astype(BF16))

        r1 = [_dot_nt(jnp.concatenate([kb[h], qs[h]], axis=0).astype(BF16), kn[h].astype(BF16))
              for h in hs]
        low = [r1[h][:chunk] * (decay[h] * strict_f) for h in hs]
        attn = [r1[h][chunk:] * decay[h] for h in hs]

        x = [eye_f - low[h] for h in hs]
        m = [_dot(low[h].astype(BF16), low[h].astype(BF16)) for h in hs]
        for _ in range(4):
            r2 = [_dot(jnp.concatenate([x[h], m[h]], axis=0).astype(BF16), m[h].astype(BF16))
                  for h in hs]
            x = [x[h] + r2[h][:chunk] for h in hs]
            m = [r2[h][chunk:] for h in hs]
        x = [x[h] + _dot(x[h].astype(BF16), m[h].astype(BF16)) for h in hs]

        uw = [_dot(x[h].astype(BF16),
                   jnp.concatenate([vb[h], kb[h] * eg[h]], axis=1).astype(BF16)) for h in hs]
        state = [s_ref[h] for h in hs]
        r3 = [_dot(jnp.concatenate([uw[h][:, hd:], qs[h] * eg[h]], axis=0).astype(BF16),
                   state[h].astype(BF16)) for h in hs]
        vnb = [(uw[h][:, :hd] - r3[h][:chunk]).astype(BF16) for h in hs]
        o = [r3[h][chunk:] + _dot(attn[h].astype(BF16), vnb[h]) for h in hs]
        outs = []
        for h in hs:
            g_last = gcol[h][chunk - 1:chunk, :]
            s_ref[h] = state[h] * jnp.exp(g_last) + _dot(kdt[h], vnb[h])
            on = o[h] * lax.rsqrt(jnp.mean(o[h] * o[h], axis=-1, keepdims=True) + RMS_EPS) * gate_w
            outs.append(on * _silu(zs[h]))

        for s in range(nb):
            o_ref[s, pl.ds(r0, chunk), :] = jnp.concatenate(
                outs[s * heads:(s + 1) * heads], axis=-1).astype(o_ref.dtype)
        return carry

    lax.fori_loop(0, rows // chunk, chunk_body, 0)


def _gdn(proj, head_params, gate_w, rows=256, nb=4):
    heads, hd = GDN_HEADS, GDN_HEAD_DIM
    width = heads * hd
    batch, seq, _ = proj.shape
    ab_block = (4 * width + MEM_HEADS * MEM_HEAD_DIM) // LANES
    return pl.pallas_call(
        functools.partial(_gdn_kernel, heads=heads, hd=hd, chunk=GDN_CHUNK),
        grid=(batch // nb, seq // rows),
        in_specs=[
            pl.BlockSpec((nb, rows, 3 * width), lambda g, i: (g, i, 0)),
            pl.BlockSpec((nb, rows, width), lambda g, i: (g, i, 3)),
            pl.BlockSpec((nb, rows, LANES), lambda g, i: (g, i, ab_block)),
            _resident((2, LANES)),
            _resident((1, hd)),
        ],
        out_specs=pl.BlockSpec((nb, rows, width), lambda g, i: (g, i, 0)),
        out_shape=jax.ShapeDtypeStruct((batch, seq, width), BF16),
        scratch_shapes=[pltpu.VMEM((nb * heads, hd, hd), F32)],
        compiler_params=_params(2),
        name="gdn",
    )(proj, proj, proj, head_params, gate_w)


def _diff_attn_kernel(q_ref, k_ref, vt_ref, lp_ref, sw_ref, o_ref, s_scr, *, tq, tk, nh, lam_init):
    i = pl.program_id(2)
    d = DIFF_HEAD_DIM
    hw = 2 * d
    chains = range(nh)

    qs = []
    for c in chains:
        q = q_ref[:, c * hw:(c + 1) * hw]
        lane = lax.broadcasted_iota(jnp.int32, q.shape, 1)
        zero = jnp.zeros_like(q)
        qs.append(jnp.concatenate([jnp.where(lane < d, q, zero), jnp.where(lane >= d, q, zero)],
                                  axis=0))

    ones_rows = jnp.ones((BF16_ROWS, tk), BF16)

    units = [(c, n) for c in chains for n in range(2 * tq // tk)]
    lower = tuple(u for u, (_, n) in enumerate(units) if n % 2 == 0)
    upper = tuple(u for u, (_, n) in enumerate(units) if n % 2 == 1)
    everyone = lower + upper
    diag_ok = (lax.broadcasted_iota(jnp.int32, (tk, tk), 0)
               <= lax.broadcasted_iota(jnp.int32, (tk, tk), 1))

    def score_stage(j, buf, which):
        r0 = pl.multiple_of(j * tk, tk)
        block_max = [None] * len(units)
        for u in which:
            c, n = units[u]
            s = _dot_nt(k_ref[pl.ds(r0, tk), c * hw:(c + 1) * hw],
                        qs[c][n * tk:(n + 1) * tk])
            s_scr[buf, c, :, n * tk:(n + 1) * tk] = s
            block_max[u] = jnp.max(s, axis=0, keepdims=True)
        return tuple(block_max)

    def softmax_stage(j, buf, block_max, states, plain, diagonal=()):
        r0 = pl.multiple_of(j * tk, tk)
        out = list(states)
        for u in plain + diagonal:
            c, n = units[u]
            m, l, acc = states[u]
            s = s_scr[buf, c, :, n * tk:(n + 1) * tk]
            if u in diagonal:
                s = jnp.where(diag_ok, s, NEG_BIG)
                bm = jnp.max(s, axis=0, keepdims=True)
            else:
                bm = block_max[u]
            m_new = jnp.maximum(m, bm)
            a = jnp.exp2(m - m_new)
            p = jnp.exp2(s - m_new).astype(BF16)
            vt1 = jnp.concatenate([vt_ref[c * hw:(c + 1) * hw, pl.ds(r0, tk)], ones_rows], axis=0)
            pv = _dot(vt1, p)
            out[u] = (m_new, a * l + pv[hw:hw + 1], a * acc + pv[:hw])
        return tuple(out)

    def pair_body(t, carry):
        max_a, states = carry
        j = 2 * t
        max_b = score_stage(j + 1, 1, everyone)
        states = softmax_stage(j, 0, max_a, states, everyone)
        max_a = score_stage(j + 2, 0, everyone)
        states = softmax_stage(j + 1, 1, max_b, states, everyone)
        return max_a, states

    init = tuple((jnp.full((1, tk), NEG_BIG, F32), jnp.zeros((1, tk), F32),
                  jnp.zeros((hw, tk), F32)) for _ in units)
    max_a, states = lax.fori_loop(0, i, pair_body, (score_stage(0, 0, everyone), init))
    score_stage(2 * i + 1, 1, upper)
    states = softmax_stage(2 * i, 0, max_a, states, upper, lower)
    states = softmax_stage(2 * i + 1, 1, None, states, (), upper)

    lp = lp_ref[...]
    lam = (jnp.exp(jnp.sum(lp[0:1] * lp[1:2], axis=-1, keepdims=True))
           - jnp.exp(jnp.sum(lp[2:3] * lp[3:4], axis=-1, keepdims=True)) + lam_init)
    for c in chains:
        mine = [states[u] for u, (cc, _) in enumerate(units) if cc == c]
        l = jnp.concatenate([st[1] for st in mine], axis=1)
        acc = jnp.concatenate([st[2] for st in mine], axis=1)
        o = acc[:, :tq] / l[:, :tq] - lam * (acc[:, tq:] / l[:, tq:])
        on = o * lax.rsqrt(jnp.mean(o * o, axis=0, keepdims=True) + RMS_EPS) * sw_ref[...]
        o_ref[:, c * hw:(c + 1) * hw] = (on * (1.0 - lam_init)).T.astype(o_ref.dtype)


def _diff_attn(proj, v_t, lam_params, subln_w, batch, seq, lam_init, tq=512, nh=2):
    heads = DIFF_HEADS
    hw = 2 * DIFF_HEAD_DIM
    tk = tq // 2
    n = batch * seq
    nq = seq // tq
    groups = heads // nh
    return pl.pallas_call(
        functools.partial(_diff_attn_kernel, tq=tq, tk=tk, nh=nh, lam_init=lam_init),
        grid=(batch, groups, nq),
        in_specs=[
            pl.BlockSpec((tq, nh * hw), lambda b, g, i: (b * nq + i, g)),
            pl.BlockSpec((seq, nh * hw), lambda b, g, i: (b, groups + g)),
            pl.BlockSpec((nh * hw, seq), lambda b, g, i: (b * groups + g, 0)),
            _resident((4, hw)),
            _resident((hw, 1)),
        ],
        out_specs=pl.BlockSpec((tq, nh * hw), lambda b, g, i: (b * nq + i, g)),
        out_shape=jax.ShapeDtypeStruct((n, heads * hw), BF16),
        scratch_shapes=[pltpu.VMEM((2, nh, tk, 2 * tq), F32)],
        compiler_params=_params(3),
        name="diff_attn",
    )(proj, proj, v_t, lam_params, subln_w)


def _out_proj_kernel(o_ref, mq_ref, kv_ref, wo_ref, wm_ref, x_ref, g_ref, b_ref, out_ref):
    mq = mq_ref[...].astype(BF16)
    kv = kv_ref[...]
    mw = MEM_HEADS * MEM_HEAD_DIM
    parts = []
    for h in range(MEM_HEADS):
        lo, hi = h * MEM_HEAD_DIM, (h + 1) * MEM_HEAD_DIM
        s = _dot_nt(mq[:, lo:hi], kv[:, lo:hi]) * (MEM_HEAD_DIM ** -0.5)
        p = jnp.exp(s - jnp.max(s, axis=-1, keepdims=True))
        pv = _dot(p.astype(BF16), kv[:, mw + lo:mw + hi])
        parts.append(pv / jnp.sum(p, axis=-1, keepdims=True))
    m = jnp.concatenate(parts, axis=-1).astype(BF16)
    mix = _dot(o_ref[...], wo_ref[...]) + _dot(m, wm_ref[...])
    y = ALPHA * x_ref[...] + mix
    out_ref[...] = _layer_norm(y, g_ref[...], b_ref[...])


def _out_proj(o, proj, mq_block, mem_kv, w_o, w_m, x2d, g, b, seq, mem_len, tm=512):
    n, d = x2d.shape
    mw = MEM_HEADS * MEM_HEAD_DIM
    per_batch = seq // tm
    return pl.pallas_call(
        _out_proj_kernel,
        grid=(n // tm,),
        in_specs=[
            pl.BlockSpec((tm, o.shape[1]), lambda i: (i, 0)),
            pl.BlockSpec((tm, mw), lambda i: (i, mq_block)),
            pl.BlockSpec((mem_len, 2 * mw), lambda i: (i // per_batch, 0)),
            _resident(w_o.shape),
            _resident(w_m.shape),
            pl.BlockSpec((tm, d), lambda i: (i, 0)),
            _resident((1, d)),
            _resident((1, d)),
        ],
        out_specs=pl.BlockSpec((tm, d), lambda i: (i, 0)),
        out_shape=jax.ShapeDtypeStruct((n, d), F32),
        compiler_params=_params(1),
        name="out_proj",
    )(o, proj, mem_kv, w_o, w_m, x2d, g, b)


def _mlp_kernel(x_ref, w1_ref, w2_ref, g_ref, b_ref, out_ref, *, ff_chunk):
    x = x_ref[...]
    xb = x.astype(BF16)
    d_ff = w1_ref.shape[1]
    acc = jnp.zeros(x.shape, F32)
    for c0 in range(0, d_ff, ff_chunk):
        h = jnp.maximum(_dot(xb, w1_ref[:, c0:c0 + ff_chunk]), 0.0)
        acc = acc + _dot((h * h).astype(BF16), w2_ref[c0:c0 + ff_chunk, :])
    out_ref[...] = _layer_norm(ALPHA * x + acc, g_ref[...], b_ref[...])


def _mlp(x2d, w1, w2, g, b, tm=512, ff_chunk=512):
    n, d = x2d.shape
    return pl.pallas_call(
        functools.partial(_mlp_kernel, ff_chunk=ff_chunk),
        grid=(n // tm,),
        in_specs=[
            pl.BlockSpec((tm, d), lambda i: (i, 0)),
            _resident(w1.shape),
            _resident(w2.shape),
            _resident((1, d)),
            _resident((1, d)),
        ],
        out_specs=pl.BlockSpec((tm, d), lambda i: (i, 0)),
        out_shape=jax.ShapeDtypeStruct((n, d), F32),
        compiler_params=_params(1),
        name="mlp",
    )(x2d, w1, w2, g, b)


def _row(v):
    return v.reshape(1, -1).astype(F32)


def _pad_lanes(v, lanes=LANES):
    return jnp.pad(v.astype(F32), (0, lanes - v.shape[0]))


def kernel(x, mem, mem_ln_g, mem_ln_b, w_mem_kv, l0_w_in, l0_conv_w, l0_a_log, l0_dt_bias, l0_gate_norm_w, l0_w_out, l0_ln1_g, l0_ln1_b, l0_w_ff1, l0_w_ff2, l0_ln2_g, l0_ln2_b, l1_w_in, l1_lambda_q1, l1_lambda_k1, l1_lambda_q2, l1_lambda_k2, l1_subln_w, l1_w_out, l1_ln1_g, l1_ln1_b, l1_w_ff1, l1_w_ff2, l1_ln2_g, l1_ln2_b):
    batch, seq, d = x.shape
    mem_len = mem.shape[1]
    n = batch * seq
    gw = GDN_HEADS * GDN_HEAD_DIM
    dw = 2 * DIFF_HEADS * DIFF_HEAD_DIM
    mw = MEM_HEADS * MEM_HEAD_DIM

    x2d = x.reshape(n, d)
    mem_kv = _mem_kv(mem.reshape(batch * mem_len, d), _row(mem_ln_g), _row(mem_ln_b),
                     w_mem_kv.astype(BF16), rows=mem_len)

    ab_cols = l0_w_in[:, 4 * gw:4 * gw + 2 * GDN_HEADS]
    w_in0 = jnp.concatenate(
        [l0_w_in[:, :4 * gw], l0_w_in[:, 4 * gw + 2 * GDN_HEADS:],
         jnp.pad(ab_cols, ((0, 0), (0, LANES - 2 * GDN_HEADS)))], axis=1).astype(BF16)
    proj0 = _in_proj_conv(x2d, w_in0, l0_conv_w.astype(F32), seq)
    head_params = jnp.stack([_pad_lanes(l0_a_log), _pad_lanes(l0_dt_bias)])
    o0 = _gdn(proj0.reshape(batch, seq, -1), head_params, _row(l0_gate_norm_w)).reshape(n, gw)
    x1 = _out_proj(o0, proj0, (4 * gw) // mw, mem_kv, l0_w_out[:gw].astype(BF16),
                   l0_w_out[gw:].astype(BF16), x2d, _row(l0_ln1_g), _row(l0_ln1_b), seq, mem_len)
    x2 = _mlp(x1, l0_w_ff1.astype(BF16), l0_w_ff2.astype(BF16), _row(l0_ln2_g), _row(l0_ln2_b))

    lam_init = 0.8 - 0.6 * math.exp(-0.3 * 1)
    q_scale = DIFF_HEAD_DIM ** -0.5 * math.log2(math.e)
    w_in1 = jnp.concatenate([l1_w_in[:, :2 * dw], l1_w_in[:, 3 * dw:]], axis=1).astype(BF16)
    w_v_t = l1_w_in[:, 2 * dw:3 * dw].T.astype(BF16)
    proj1, v_t = _in_proj(x2, w_in1, w_v_t, seq, lead_cols=dw, lead_scale=q_scale)
    lam_params = jnp.stack([_pad_lanes(v) for v in
                            (l1_lambda_q1, l1_lambda_k1, l1_lambda_q2, l1_lambda_k2)])
    o1 = _diff_attn(proj1, v_t.reshape(batch * dw, seq), lam_params,
                    l1_subln_w.reshape(-1, 1).astype(F32), batch, seq, lam_init)
    x3 = _out_proj(o1, proj1, (2 * dw) // mw, mem_kv, l1_w_out[:dw].astype(BF16),
                   l1_w_out[dw:].astype(BF16), x2, _row(l1_ln1_g), _row(l1_ln1_b), seq, mem_len)
    x4 = _mlp(x3, l1_w_ff1.astype(BF16), l1_w_ff2.astype(BF16), _row(l1_ln2_g), _row(l1_ln2_b))
    return x4.reshape(batch, seq, d)
```

```python
import functools
import math

import jax
import jax.numpy as jnp
from jax import lax
from jax.experimental import pallas as pl
from jax.experimental.pallas import tpu as pltpu

F32 = jnp.float32
BF16 = jnp.bfloat16

DEPTH = 2
GDN_HEADS = 8
GDN_HEAD_DIM = 128
CONV_WIDTH = 4
GDN_CHUNK = 64
DIFF_HEADS = 8
DIFF_HEAD_DIM = 64
MEM_HEADS = 4
MEM_HEAD_DIM = 128
ALPHA = (2.0 * DEPTH) ** 0.25
LN_EPS = 1e-5
RMS_EPS = 1e-6

LANES = 128
SUBLANES = 8
BF16_ROWS = 16
VMEM_LIMIT_BYTES = 56 * 1024 * 1024

NEG_BIG = -1e30


def _params(n_axes):
    return pltpu.CompilerParams(
        dimension_semantics=("arbitrary",) * n_axes,
        vmem_limit_bytes=VMEM_LIMIT_BYTES,
    )


def _resident(shape):
    nd = len(shape)
    return pl.BlockSpec(shape, lambda *_: (0,) * nd, pipeline_mode=pl.Buffered(1))


def _dot(a, b):
    return jnp.dot(a, b, preferred_element_type=F32)


def _dot_nt(a, b):
    return lax.dot_general(a, b, (((1,), (1,)), ((), ())), preferred_element_type=F32)


def _layer_norm(y, g, b):
    mu = jnp.mean(y, axis=-1, keepdims=True)
    yc = y - mu
    var = jnp.mean(yc * yc, axis=-1, keepdims=True)
    return yc * lax.rsqrt(var + LN_EPS) * g + b


def _silu(x):
    return x * jax.nn.sigmoid(x)


def _softplus(x):
    return jnp.maximum(x, 0.0) + jnp.log1p(jnp.exp(-jnp.abs(x)))


def _mem_kv_kernel(mem_ref, g_ref, b_ref, w_ref, o_ref):
    y = _layer_norm(mem_ref[...], g_ref[...], b_ref[...])
    o_ref[...] = _dot(y.astype(BF16), w_ref[...]).astype(o_ref.dtype)


def _mem_kv(mem2d, g, b, w, rows):
    n, d = mem2d.shape
    nout = w.shape[1]
    return pl.pallas_call(
        _mem_kv_kernel,
        grid=(n // rows,),
        in_specs=[
            pl.BlockSpec((rows, d), lambda i: (i, 0)),
            _resident((1, d)),
            _resident((1, d)),
            _resident((d, nout)),
        ],
        out_specs=pl.BlockSpec((rows, nout), lambda i: (i, 0)),
        out_shape=jax.ShapeDtypeStruct((n, nout), BF16),
        compiler_params=_params(1),
        name="mem_kv",
    )(mem2d, g, b, w)


def _in_proj_kernel(x_ref, w_ref, o_ref, ot_ref, *, col_chunk, lead_cols, lead_scale, t_cols):
    xb = x_ref[...].astype(BF16)
    oc = 0
    for c0 in range(0, w_ref.shape[1], col_chunk):
        c1 = c0 + col_chunk
        acc = _dot(xb, w_ref[:, c0:c1])
        if t_cols[0] <= c0 and c1 <= t_cols[1]:
            r0 = c0 - t_cols[0]
            ot_ref[0, r0:r0 + col_chunk, :] = acc.T.astype(ot_ref.dtype)
        else:
            if c1 <= lead_cols:
                acc = acc * lead_scale
            o_ref[:, oc:oc + col_chunk] = acc.astype(o_ref.dtype)
            oc += col_chunk


def _in_proj_conv_kernel(x_ref, w_ref, cw_ref, o_ref, tail_ref, *, col_chunk, conv_cols,
                         tiles_per_seq):
    @pl.when(pl.program_id(0) == 0)
    def _():
        tail_ref[...] = jnp.zeros_like(tail_ref)

    xb = x_ref[...].astype(BF16)
    tm = x_ref.shape[0]
    nout = o_ref.shape[1]
    seq_start = (pl.program_id(0) % tiles_per_seq) == 0
    for c0 in range(0, nout, col_chunk):
        c1 = min(c0 + col_chunk, nout)
        acc = _dot(xb, w_ref[:, c0:c1])
        if c1 <= conv_cols:
            tail = jnp.where(seq_start, 0.0, tail_ref[:, c0:c1])
            ext = jnp.concatenate([tail, acc], axis=0)
            cw = cw_ref[:, c0:c1]
            conv = cw[CONV_WIDTH - 1:CONV_WIDTH] * acc
            for j in range(CONV_WIDTH - 1):
                off = SUBLANES - (CONV_WIDTH - 1) + j
                conv = conv + cw[j:j + 1] * ext[off:off + tm]
            tail_ref[:, c0:c1] = acc[tm - SUBLANES:tm]
            acc = _silu(conv)
        o_ref[:, c0:c1] = acc.astype(o_ref.dtype)


def _in_proj_conv(x2d, w, conv_w, seq, tm=512, col_chunk=512):
    n, d = x2d.shape
    nout = w.shape[1]
    conv_cols = conv_w.shape[1]
    assert conv_cols % col_chunk == 0 and seq % tm == 0
    return pl.pallas_call(
        functools.partial(_in_proj_conv_kernel, col_chunk=col_chunk, conv_cols=conv_cols,
                          tiles_per_seq=seq // tm),
        grid=(n // tm,),
        in_specs=[pl.BlockSpec((tm, d), lambda i: (i, 0)), _resident((d, nout)),
                  _resident(conv_w.shape)],
        out_specs=pl.BlockSpec((tm, nout), lambda i: (i, 0)),
        out_shape=jax.ShapeDtypeStruct((n, nout), F32),
        scratch_shapes=[pltpu.VMEM((SUBLANES, conv_cols), F32)],
        compiler_params=_params(1),
        name="in_proj_conv",
    )(x2d, w, conv_w)


def _in_proj(x2d, w, seq, t_cols, tm=512, col_chunk=512, lead_cols=0, lead_scale=1.0):
    n, d = x2d.shape
    nt = t_cols[1] - t_cols[0]
    nout = w.shape[1] - nt
    per_seq = seq // tm
    assert lead_cols <= t_cols[0] and all(c % col_chunk == 0 for c in (lead_cols, *t_cols, nout))
    return pl.pallas_call(
        functools.partial(_in_proj_kernel, col_chunk=col_chunk, lead_cols=lead_cols,
                          lead_scale=lead_scale, t_cols=t_cols),
        grid=(n // tm,),
        in_specs=[pl.BlockSpec((tm, d), lambda i: (i, 0)), _resident(w.shape)],
        out_specs=[pl.BlockSpec((tm, nout), lambda i: (i, 0)),
                   pl.BlockSpec((1, nt, tm), lambda i: (i // per_seq, 0, i % per_seq))],
        out_shape=[jax.ShapeDtypeStruct((n, nout), BF16),
                   jax.ShapeDtypeStruct((n // seq, nt, seq), BF16)],
        compiler_params=_params(1),
        name="in_proj",
    )(x2d, w)


def _gdn_kernel(qkv_ref, z_ref, ab_ref, hp_ref, gw_ref, o_ref, s_ref, *, heads, hd, chunk):
    @pl.when(pl.program_id(1) == 0)
    def _():
        s_ref[...] = jnp.zeros_like(s_ref)

    nb, rows = qkv_ref.shape[0], qkv_ref.shape[1]
    width = heads * hd
    neg_a = -jnp.exp(hp_ref[0:1, :])
    dt_bias = hp_ref[1:2, :]
    gate_w = gw_ref[...]
    scale = hd ** -0.5

    ri = lax.broadcasted_iota(jnp.int32, (chunk, chunk), 0)
    ci = lax.broadcasted_iota(jnp.int32, (chunk, chunk), 1)
    eye = ri == ci
    tri = (ri >= ci).astype(F32)
    strict_f = (ri > ci).astype(F32)
    eye_f = eye.astype(F32)

    def chunk_body(c, carry):
        r0 = pl.multiple_of(c * chunk, chunk)
        hs = range(nb * heads)
        kn, qs, vb, kb, eg, gcol, decay, zs, kdt = [], [], [], [], [], [], [], [], []
        for s in range(nb):
            act = qkv_ref[s, pl.ds(r0, chunk), :]
            ab = ab_ref[s, pl.ds(r0, chunk), :]
            g_all = neg_a * _softplus(ab + dt_bias)
            beta_all = jax.nn.sigmoid(ab)
            gc_all = jnp.dot(tri, g_all, precision=lax.Precision.HIGHEST,
                             preferred_element_type=F32)
            zc = z_ref[s, pl.ds(r0, chunk), :]

            for h in range(heads):
                qh = act[:, h * hd:(h + 1) * hd]
                kh = act[:, width + h * hd:width + (h + 1) * hd]
                vh = act[:, 2 * width + h * hd:2 * width + (h + 1) * hd]
                qn = qh * lax.rsqrt(jnp.sum(qh * qh, axis=-1, keepdims=True) + RMS_EPS)
                kn.append(kh * lax.rsqrt(jnp.sum(kh * kh, axis=-1, keepdims=True) + RMS_EPS))
                gc = gc_all[:, h:h + 1]
                bcol = beta_all[:, heads + h:heads + h + 1]
                g_i = jnp.broadcast_to(gc, (chunk, chunk))
                g_j = jnp.sum(jnp.where(eye, g_i, 0.0), axis=0, keepdims=True)
                decay.append(tri * jnp.exp(jnp.minimum(g_i - g_j, 0.0)))
                gcol.append(gc)
                eg.append(jnp.exp(gc))
                kb.append(kn[-1] * bcol)
                vb.append(vh * bcol)
                qs.append(qn * scale)
                zs.append(zc[:, h * hd:(h + 1) * hd])
                kdt.append((kn[-1] * jnp.exp(gc[chunk - 1:chunk, :] - gc)).T.astype(BF16))

        r1 = [_dot_nt(jnp.concatenate([kb[h], qs[h]], axis=0).astype(BF16), kn[h].astype(BF16))
              for h in hs]
        low = [r1[h][:chunk] * (decay[h] * strict_f) for h in hs]
        attn = [r1[h][chunk:] * decay[h] for h in hs]

        x = [eye_f - low[h] for h in hs]
        m = [_dot(low[h].astype(BF16), low[h].astype(BF16)) for h in hs]
        for _ in range(4):
            r2 = [_dot(jnp.concatenate([x[h], m[h]], axis=0).astype(BF16), m[h].astype(BF16))
                  for h in hs]
            x = [x[h] + r2[h][:chunk] for h in hs]
            m = [r2[h][chunk:] for h in hs]
        x = [x[h] + _dot(x[h].astype(BF16), m[h].astype(BF16)) for h in hs]

        uw = [_dot(x[h].astype(BF16),
                   jnp.concatenate([vb[h], kb[h] * eg[h]], axis=1).astype(BF16)) for h in hs]
        state = [s_ref[h] for h in hs]
        r3 = [_dot(jnp.concatenate([uw[h][:, hd:], qs[h] * eg[h]], axis=0).astype(BF16),
                   state[h].astype(BF16)) for h in hs]
        vnb = [(uw[h][:, :hd] - r3[h][:chunk]).astype(BF16) for h in hs]
        o = [r3[h][chunk:] + _dot(attn[h].astype(BF16), vnb[h]) for h in hs]
        outs = []
        for h in hs:
            g_last = gcol[h][chunk - 1:chunk, :]
            s_ref[h] = state[h] * jnp.exp(g_last) + _dot(kdt[h], vnb[h])
            on = o[h] * lax.rsqrt(jnp.mean(o[h] * o[h], axis=-1, keepdims=True) + RMS_EPS) * gate_w
            outs.append(on * _silu(zs[h]))

        for s in range(nb):
            o_ref[s, pl.ds(r0, chunk), :] = jnp.concatenate(
                outs[s * heads:(s + 1) * heads], axis=-1).astype(o_ref.dtype)
        return carry

    lax.fori_loop(0, rows // chunk, chunk_body, 0)


def _gdn(proj, head_params, gate_w, rows=256, nb=4):
    heads, hd = GDN_HEADS, GDN_HEAD_DIM
    width = heads * hd
    batch, seq, _ = proj.shape
    ab_block = (4 * width + MEM_HEADS * MEM_HEAD_DIM) // LANES
    return pl.pallas_call(
        functools.partial(_gdn_kernel, heads=heads, hd=hd, chunk=GDN_CHUNK),
        grid=(batch // nb, seq // rows),
        in_specs=[
            pl.BlockSpec((nb, rows, 3 * width), lambda g, i: (g, i, 0)),
            pl.BlockSpec((nb, rows, width), lambda g, i: (g, i, 3)),
            pl.BlockSpec((nb, rows, LANES), lambda g, i: (g, i, ab_block)),
            _resident((2, LANES)),
            _resident((1, hd)),
        ],
        out_specs=pl.BlockSpec((nb, rows, width), lambda g, i: (g, i, 0)),
        out_shape=jax.ShapeDtypeStruct((batch, seq, width), BF16),
        scratch_shapes=[pltpu.VMEM((nb * heads, hd, hd), F32)],
        compiler_params=_params(2),
        name="gdn",
    )(proj, proj, proj, head_params, gate_w)


def _diff_attn_kernel(q_ref, k_ref, vt_ref, lp_ref, sw_ref, o_ref, s_scr, *, tq, tk, nh, lam_init):
    i = pl.program_id(2)
    d = DIFF_HEAD_DIM
    hw = 2 * d
    chains = range(nh)

    qs = []
    for c in chains:
        q = q_ref[:, c * hw:(c + 1) * hw]
        lane = lax.broadcasted_iota(jnp.int32, q.shape, 1)
        zero = jnp.zeros_like(q)
        qs.append(jnp.concatenate([jnp.where(lane < d, q, zero), jnp.where(lane >= d, q, zero)],
                                  axis=0))

    ones_rows = jnp.ones((BF16_ROWS, tk), BF16)

    units = [(c, n) for c in chains for n in range(2 * tq // tk)]
    lower = tuple(u for u, (_, n) in enumerate(units) if n % 2 == 0)
    upper = tuple(u for u, (_, n) in enumerate(units) if n % 2 == 1)
    everyone = lower + upper
    diag_ok = (lax.broadcasted_iota(jnp.int32, (tk, tk), 0)
               <= lax.broadcasted_iota(jnp.int32, (tk, tk), 1))

    def score_stage(j, buf, which):
        r0 = pl.multiple_of(j * tk, tk)
        block_max = [None] * len(units)
        for u in which:
            c, n = units[u]
            s = _dot_nt(k_ref[pl.ds(r0, tk), c * hw:(c + 1) * hw],
                        qs[c][n * tk:(n + 1) * tk])
            s_scr[buf, c, :, n * tk:(n + 1) * tk] = s
            block_max[u] = jnp.max(s, axis=0, keepdims=True)
        return tuple(block_max)

    def softmax_stage(j, buf, block_max, states, plain, diagonal=()):
        r0 = pl.multiple_of(j * tk, tk)
        out = list(states)
        for u in plain + diagonal:
            c, n = units[u]
            m, l, acc = states[u]
            s = s_scr[buf, c, :, n * tk:(n + 1) * tk]
            if u in diagonal:
                s = jnp.where(diag_ok, s, NEG_BIG)
                bm = jnp.max(s, axis=0, keepdims=True)
            else:
                bm = block_max[u]
            m_new = jnp.maximum(m, bm)
            a = jnp.exp2(m - m_new)
            p = jnp.exp2(s - m_new).astype(BF16)
            vt1 = jnp.concatenate([vt_ref[c * hw:(c + 1) * hw, pl.ds(r0, tk)], ones_rows], axis=0)
            pv = _dot(vt1, p)
            out[u] = (m_new, a * l + pv[hw:hw + 1], a * acc + pv[:hw])
        return tuple(out)

    def pair_body(t, carry):
        max_a, states = carry
        j = 2 * t
        max_b = score_stage(j + 1, 1, everyone)
        states = softmax_stage(j, 0, max_a, states, everyone)
        max_a = score_stage(j + 2, 0, everyone)
        states = softmax_stage(j + 1, 1, max_b, states, everyone)
        return max_a, states

    init = tuple((jnp.full((1, tk), NEG_BIG, F32), jnp.zeros((1, tk), F32),
                  jnp.zeros((hw, tk), F32)) for _ in units)
    max_a, states = lax.fori_loop(0, i, pair_body, (score_stage(0, 0, everyone), init))
    score_stage(2 * i + 1, 1, upper)
    states = softmax_stage(2 * i, 0, max_a, states, upper, lower)
    states = softmax_stage(2 * i + 1, 1, None, states, (), upper)

    lp = lp_ref[...]
    lam = (jnp.exp(jnp.sum(lp[0:1] * lp[1:2], axis=-1, keepdims=True))
           - jnp.exp(jnp.sum(lp[2:3] * lp[3:4], axis=-1, keepdims=True)) + lam_init)
    for c in chains:
        mine = [states[u] for u, (cc, _) in enumerate(units) if cc == c]
        l = jnp.concatenate([st[1] for st in mine], axis=1)
        acc = jnp.concatenate([st[2] for st in mine], axis=1)
        o = acc[:, :tq] / l[:, :tq] - lam * (acc[:, tq:] / l[:, tq:])
        on = o * lax.rsqrt(jnp.mean(o * o, axis=0, keepdims=True) + RMS_EPS) * sw_ref[...]
        o_ref[:, c * hw:(c + 1) * hw] = (on * (1.0 - lam_init)).T.astype(o_ref.dtype)


def _diff_attn(proj, v_t, lam_params, subln_w, batch, seq, lam_init, tq=1024, nh=1):
    heads = DIFF_HEADS
    hw = 2 * DIFF_HEAD_DIM
    tk = tq // 2
    n = batch * seq
    nq = seq // tq
    groups = heads // nh
    return pl.pallas_call(
        functools.partial(_diff_attn_kernel, tq=tq, tk=tk, nh=nh, lam_init=lam_init),
        grid=(batch, groups, nq),
        in_specs=[
            pl.BlockSpec((tq, nh * hw), lambda b, g, i: (b * nq + i, g)),
            pl.BlockSpec((seq, nh * hw), lambda b, g, i: (b, groups + g)),
            pl.BlockSpec((nh * hw, seq), lambda b, g, i: (b * groups + g, 0)),
            _resident((4, hw)),
            _resident((hw, 1)),
        ],
        out_specs=pl.BlockSpec((tq, nh * hw), lambda b, g, i: (b * nq + i, g)),
        out_shape=jax.ShapeDtypeStruct((n, heads * hw), BF16),
        scratch_shapes=[pltpu.VMEM((2, nh, tk, 2 * tq), F32)],
        compiler_params=_params(3),
        name="diff_attn",
    )(proj, proj, v_t, lam_params, subln_w)


def _out_proj_kernel(o_ref, mq_ref, kv_ref, w_ref, x_ref, g_ref, b_ref, out_ref):
    ow = o_ref.shape[1]
    mq = mq_ref[...].astype(BF16)
    kv = kv_ref[...]
    mw = MEM_HEADS * MEM_HEAD_DIM
    parts = []
    for h in range(MEM_HEADS):
        lo, hi = h * MEM_HEAD_DIM, (h + 1) * MEM_HEAD_DIM
        s = _dot_nt(mq[:, lo:hi], kv[:, lo:hi]) * (MEM_HEAD_DIM ** -0.5)
        p = jnp.exp(s - jnp.max(s, axis=-1, keepdims=True))
        pv = _dot(p.astype(BF16), kv[:, mw + lo:mw + hi])
        parts.append(pv / jnp.sum(p, axis=-1, keepdims=True))
    m = jnp.concatenate(parts, axis=-1).astype(BF16)
    mix = _dot(o_ref[...], w_ref[:ow, :]) + _dot(m, w_ref[ow:, :])
    y = ALPHA * x_ref[...] + mix
    out_ref[...] = _layer_norm(y, g_ref[...], b_ref[...])


def _out_proj(o, proj, mq_block, mem_kv, w, x2d, g, b, seq, mem_len, tm=512):
    n, d = x2d.shape
    mw = MEM_HEADS * MEM_HEAD_DIM
    per_batch = seq // tm
    return pl.pallas_call(
        _out_proj_kernel,
        grid=(n // tm,),
        in_specs=[
            pl.BlockSpec((tm, o.shape[1]), lambda i: (i, 0)),
            pl.BlockSpec((tm, mw), lambda i: (i, mq_block)),
            pl.BlockSpec((mem_len, 2 * mw), lambda i: (i // per_batch, 0)),
            _resident(w.shape),
            pl.BlockSpec((tm, d), lambda i: (i, 0)),
            _resident((1, d)),
            _resident((1, d)),
        ],
        out_specs=pl.BlockSpec((tm, d), lambda i: (i, 0)),
        out_shape=jax.ShapeDtypeStruct((n, d), F32),
        compiler_params=_params(1),
        name="out_proj",
    )(o, proj, mem_kv, w, x2d, g, b)


def _mlp_kernel(x_ref, w1_ref, w2_ref, g_ref, b_ref, out_ref, *, ff_chunk):
    x = x_ref[...]
    xb = x.astype(BF16)
    d_ff = w1_ref.shape[1]
    acc = jnp.zeros(x.shape, F32)
    for c0 in range(0, d_ff, ff_chunk):
        h = jnp.maximum(_dot(xb, w1_ref[:, c0:c0 + ff_chunk]), 0.0)
        acc = acc + _dot((h * h).astype(BF16), w2_ref[c0:c0 + ff_chunk, :])
    out_ref[...] = _layer_norm(ALPHA * x + acc, g_ref[...], b_ref[...])


def _mlp(x2d, w1, w2, g, b, tm=512, ff_chunk=512):
    n, d = x2d.shape
    return pl.pallas_call(
        functools.partial(_mlp_kernel, ff_chunk=ff_chunk),
        grid=(n // tm,),
        in_specs=[
            pl.BlockSpec((tm, d), lambda i: (i, 0)),
            _resident(w1.shape),
            _resident(w2.shape),
            _resident((1, d)),
            _resident((1, d)),
        ],
        out_specs=pl.BlockSpec((tm, d), lambda i: (i, 0)),
        out_shape=jax.ShapeDtypeStruct((n, d), F32),
        compiler_params=_params(1),
        name="mlp",
    )(x2d, w1, w2, g, b)


def _row(v):
    return v.reshape(1, -1).astype(F32)


def _pad_lanes(v, lanes=LANES):
    return jnp.pad(v.astype(F32), (0, lanes - v.shape[0]))


def kernel(x, mem, mem_ln_g, mem_ln_b, w_mem_kv, l0_w_in, l0_conv_w, l0_a_log, l0_dt_bias, l0_gate_norm_w, l0_w_out, l0_ln1_g, l0_ln1_b, l0_w_ff1, l0_w_ff2, l0_ln2_g, l0_ln2_b, l1_w_in, l1_lambda_q1, l1_lambda_k1, l1_lambda_q2, l1_lambda_k2, l1_subln_w, l1_w_out, l1_ln1_g, l1_ln1_b, l1_w_ff1, l1_w_ff2, l1_ln2_g, l1_ln2_b):
    batch, seq, d = x.shape
    mem_len = mem.shape[1]
    n = batch * seq
    gw = GDN_HEADS * GDN_HEAD_DIM
    dw = 2 * DIFF_HEADS * DIFF_HEAD_DIM
    mw = MEM_HEADS * MEM_HEAD_DIM

    x2d = x.reshape(n, d)
    mem_kv = _mem_kv(mem.reshape(batch * mem_len, d), _row(mem_ln_g), _row(mem_ln_b),
                     w_mem_kv.astype(BF16), rows=mem_len)

    ab_cols = l0_w_in[:, 4 * gw:4 * gw + 2 * GDN_HEADS]
    w_in0 = jnp.concatenate(
        [l0_w_in[:, :4 * gw], l0_w_in[:, 4 * gw + 2 * GDN_HEADS:],
         jnp.pad(ab_cols, ((0, 0), (0, LANES - 2 * GDN_HEADS)))], axis=1).astype(BF16)
    proj0 = _in_proj_conv(x2d, w_in0, l0_conv_w.astype(F32), seq)
    head_params = jnp.stack([_pad_lanes(l0_a_log), _pad_lanes(l0_dt_bias)])
    o0 = _gdn(proj0.reshape(batch, seq, -1), head_params, _row(l0_gate_norm_w)).reshape(n, gw)
    x1 = _out_proj(o0, proj0, (4 * gw) // mw, mem_kv, l0_w_out.astype(BF16), x2d,
                   _row(l0_ln1_g), _row(l0_ln1_b), seq, mem_len)
    x2 = _mlp(x1, l0_w_ff1.astype(BF16), l0_w_ff2.astype(BF16), _row(l0_ln2_g), _row(l0_ln2_b))

    lam_init = 0.8 - 0.6 * math.exp(-0.3 * 1)
    q_scale = DIFF_HEAD_DIM ** -0.5 * math.log2(math.e)
    proj1, v_t = _in_proj(x2, l1_w_in.astype(BF16), seq, (2 * dw, 3 * dw), lead_cols=dw,
                          lead_scale=q_scale)
    lam_params = jnp.stack([_pad_lanes(v) for v in
                            (l1_lambda_q1, l1_lambda_k1, l1_lambda_q2, l1_lambda_k2)])
    o1 = _diff_attn(proj1, v_t.reshape(batch * dw, seq), lam_params,
                    l1_subln_w.reshape(-1, 1).astype(F32), batch, seq, lam_init)
    x3 = _out_proj(o1, proj1, (2 * dw) // mw, mem_kv, l1_w_out.astype(BF16), x2,
                   _row(l1_ln1_g), _row(l1_ln1_b), seq, mem_len)
    x4 = _mlp(x3, l1_w_ff1.astype(BF16), l1_w_ff2.astype(BF16), _row(l1_ln2_g), _row(l1_ln2_b))
    return x4.reshape(batch, seq, d)
```

```python
import functools
import math

import jax
import jax.numpy as jnp
from jax import lax
from jax.experimental import pallas as pl
from jax.experimental.pallas import tpu as pltpu

F32 = jnp.float32
BF16 = jnp.bfloat16

DEPTH = 2
GDN_HEADS = 8
GDN_HEAD_DIM = 128
CONV_WIDTH = 4
GDN_CHUNK = 64
DIFF_HEADS = 8
DIFF_HEAD_DIM = 64
MEM_HEADS = 4
MEM_HEAD_DIM = 128
ALPHA = (2.0 * DEPTH) ** 0.25
LN_EPS = 1e-5
RMS_EPS = 1e-6

LANES = 128
SUBLANES = 8
BF16_ROWS = 16
VMEM_LIMIT_BYTES = 56 * 1024 * 1024

NEG_BIG = -1e30


def _params(n_axes):
    return pltpu.CompilerParams(
        dimension_semantics=("arbitrary",) * n_axes,
        vmem_limit_bytes=VMEM_LIMIT_BYTES,
    )


def _resident(shape):
    nd = len(shape)
    return pl.BlockSpec(shape, lambda *_: (0,) * nd, pipeline_mode=pl.Buffered(1))


def _dot(a, b):
    return jnp.dot(a, b, preferred_element_type=F32)


def _dot_nt(a, b):
    return lax.dot_general(a, b, (((1,), (1,)), ((), ())), preferred_element_type=F32)


def _layer_norm(y, g, b):
    mu = jnp.mean(y, axis=-1, keepdims=True)
    yc = y - mu
    var = jnp.mean(yc * yc, axis=-1, keepdims=True)
    return yc * lax.rsqrt(var + LN_EPS) * g + b


def _silu(x):
    return x * jax.nn.sigmoid(x)


def _softplus(x):
    return jnp.maximum(x, 0.0) + jnp.log1p(jnp.exp(-jnp.abs(x)))


def _mem_kv_kernel(mem_ref, g_ref, b_ref, w_ref, o_ref):
    y = _layer_norm(mem_ref[...], g_ref[...], b_ref[...])
    o_ref[...] = _dot(y.astype(BF16), w_ref[...].astype(BF16)).astype(o_ref.dtype)


def _mem_kv(mem2d, g, b, w, rows):
    n, d = mem2d.shape
    nout = w.shape[1]
    return pl.pallas_call(
        _mem_kv_kernel,
        grid=(n // rows,),
        in_specs=[
            pl.BlockSpec((rows, d), lambda i: (i, 0)),
            _resident((1, d)),
            _resident((1, d)),
            _resident((d, nout)),
        ],
        out_specs=pl.BlockSpec((rows, nout), lambda i: (i, 0)),
        out_shape=jax.ShapeDtypeStruct((n, nout), BF16),
        compiler_params=_params(1),
        name="mem_kv",
    )(mem2d, g, b, w)


def _in_proj_kernel(x_ref, w_ref, o_ref, ot_ref, *, col_chunk, lead_cols, lead_scale, t_cols):
    xb = x_ref[...].astype(BF16)
    oc = 0
    for c0 in range(0, w_ref.shape[1], col_chunk):
        c1 = c0 + col_chunk
        acc = _dot(xb, w_ref[:, c0:c1].astype(BF16))
        if t_cols[0] <= c0 and c1 <= t_cols[1]:
            r0 = c0 - t_cols[0]
            ot_ref[0, r0:r0 + col_chunk, :] = acc.T.astype(ot_ref.dtype)
        else:
            if c1 <= lead_cols:
                acc = acc * lead_scale
            o_ref[:, oc:oc + col_chunk] = acc.astype(o_ref.dtype)
            oc += col_chunk


def _in_proj_conv_kernel(x_ref, w_ref, cw_ref, o_ref, tail_ref, *, col_chunk, conv_cols,
                         tiles_per_seq):
    @pl.when(pl.program_id(0) == 0)
    def _():
        tail_ref[...] = jnp.zeros_like(tail_ref)

    xb = x_ref[...].astype(BF16)
    tm = x_ref.shape[0]
    nout = o_ref.shape[1]
    seq_start = (pl.program_id(0) % tiles_per_seq) == 0
    for c0 in range(0, nout, col_chunk):
        c1 = min(c0 + col_chunk, nout)
        acc = _dot(xb, w_ref[:, c0:c1])
        if c1 <= conv_cols:
            tail = jnp.where(seq_start, 0.0, tail_ref[:, c0:c1])
            ext = jnp.concatenate([tail, acc], axis=0)
            cw = cw_ref[:, c0:c1]
            conv = cw[CONV_WIDTH - 1:CONV_WIDTH] * acc
            for j in range(CONV_WIDTH - 1):
                off = SUBLANES - (CONV_WIDTH - 1) + j
                conv = conv + cw[j:j + 1] * ext[off:off + tm]
            tail_ref[:, c0:c1] = acc[tm - SUBLANES:tm]
            acc = _silu(conv)
        o_ref[:, c0:c1] = acc.astype(o_ref.dtype)


def _in_proj_conv(x2d, w, conv_w, seq, tm=512, col_chunk=512):
    n, d = x2d.shape
    nout = w.shape[1]
    conv_cols = conv_w.shape[1]
    assert conv_cols % col_chunk == 0 and seq % tm == 0
    return pl.pallas_call(
        functools.partial(_in_proj_conv_kernel, col_chunk=col_chunk, conv_cols=conv_cols,
                          tiles_per_seq=seq // tm),
        grid=(n // tm,),
        in_specs=[pl.BlockSpec((tm, d), lambda i: (i, 0)), _resident((d, nout)),
                  _resident(conv_w.shape)],
        out_specs=pl.BlockSpec((tm, nout), lambda i: (i, 0)),
        out_shape=jax.ShapeDtypeStruct((n, nout), F32),
        scratch_shapes=[pltpu.VMEM((SUBLANES, conv_cols), F32)],
        compiler_params=_params(1),
        name="in_proj_conv",
    )(x2d, w, conv_w)


def _in_proj(x2d, w, seq, t_cols, tm=512, col_chunk=512, lead_cols=0, lead_scale=1.0):
    n, d = x2d.shape
    nt = t_cols[1] - t_cols[0]
    nout = w.shape[1] - nt
    per_seq = seq // tm
    assert lead_cols <= t_cols[0] and all(c % col_chunk == 0 for c in (lead_cols, *t_cols, nout))
    return pl.pallas_call(
        functools.partial(_in_proj_kernel, col_chunk=col_chunk, lead_cols=lead_cols,
                          lead_scale=lead_scale, t_cols=t_cols),
        grid=(n // tm,),
        in_specs=[pl.BlockSpec((tm, d), lambda i: (i, 0)), _resident(w.shape)],
        out_specs=[pl.BlockSpec((tm, nout), lambda i: (i, 0)),
                   pl.BlockSpec((1, nt, tm), lambda i: (i // per_seq, 0, i % per_seq))],
        out_shape=[jax.ShapeDtypeStruct((n, nout), BF16),
                   jax.ShapeDtypeStruct((n // seq, nt, seq), BF16)],
        compiler_params=_params(1),
        name="in_proj",
    )(x2d, w)


def _gdn_kernel(qkv_ref, z_ref, ab_ref, hp_ref, gw_ref, o_ref, s_ref, *, heads, hd, chunk):
    @pl.when(pl.program_id(1) == 0)
    def _():
        s_ref[...] = jnp.zeros_like(s_ref)

    nb, rows = qkv_ref.shape[0], qkv_ref.shape[1]
    width = heads * hd
    neg_a = -jnp.exp(hp_ref[0:1, :])
    dt_bias = hp_ref[1:2, :]
    gate_w = gw_ref[...]
    scale = hd ** -0.5

    ri = lax.broadcasted_iota(jnp.int32, (chunk, chunk), 0)
    ci = lax.broadcasted_iota(jnp.int32, (chunk, chunk), 1)
    eye = ri == ci
    tri = (ri >= ci).astype(F32)
    strict_f = (ri > ci).astype(F32)
    eye_f = eye.astype(F32)

    def chunk_body(c, carry):
        r0 = pl.multiple_of(c * chunk, chunk)
        hs = range(nb * heads)
        kn, qs, vb, kb, eg, gcol, decay, zs, kdt = [], [], [], [], [], [], [], [], []
        for s in range(nb):
            act = qkv_ref[s, pl.ds(r0, chunk), :]
            ab = ab_ref[s, pl.ds(r0, chunk), :]
            g_all = neg_a * _softplus(ab + dt_bias)
            beta_all = jax.nn.sigmoid(ab)
            gc_all = jnp.dot(tri, g_all, precision=lax.Precision.HIGHEST,
                             preferred_element_type=F32)
            zc = z_ref[s, pl.ds(r0, chunk), :]

            for h in range(heads):
                qh = act[:, h * hd:(h + 1) * hd]
                kh = act[:, width + h * hd:width + (h + 1) * hd]
                vh = act[:, 2 * width + h * hd:2 * width + (h + 1) * hd]
                qn = qh * lax.rsqrt(jnp.sum(qh * qh, axis=-1, keepdims=True) + RMS_EPS)
                kn.append(kh * lax.rsqrt(jnp.sum(kh * kh, axis=-1, keepdims=True) + RMS_EPS))
                gc = gc_all[:, h:h + 1]
                bcol = beta_all[:, heads + h:heads + h + 1]
                g_i = jnp.broadcast_to(gc, (chunk, chunk))
                g_j = jnp.sum(jnp.where(eye, g_i, 0.0), axis=0, keepdims=True)
                decay.append(tri * jnp.exp(jnp.minimum(g_i - g_j, 0.0)))
                gcol.append(gc)
                eg.append(jnp.exp(gc))
                kb.append(kn[-1] * bcol)
                vb.append(vh * bcol)
                qs.append(qn * scale)
                zs.append(zc[:, h * hd:(h + 1) * hd])
                kdt.append((kn[-1] * jnp.exp(gc[chunk - 1:chunk, :] - gc)).T.astype(BF16))

        r1 = [_dot_nt(jnp.concatenate([kb[h], qs[h]], axis=0).astype(BF16), kn[h].astype(BF16))
              for h in hs]
        low = [r1[h][:chunk] * (decay[h] * strict_f) for h in hs]
        attn = [r1[h][chunk:] * decay[h] for h in hs]

        x = [eye_f - low[h] for h in hs]
        m = [_dot(low[h].astype(BF16), low[h].astype(BF16)) for h in hs]
        for _ in range(4):
            r2 = [_dot(jnp.concatenate([x[h], m[h]], axis=0).astype(BF16), m[h].astype(BF16))
                  for h in hs]
            x = [x[h] + r2[h][:chunk] for h in hs]
            m = [r2[h][chunk:] for h in hs]
        x = [x[h] + _dot(x[h].astype(BF16), m[h].astype(BF16)) for h in hs]

        uw = [_dot(x[h].astype(BF16),
                   jnp.concatenate([vb[h], kb[h] * eg[h]], axis=1).astype(BF16)) for h in hs]
        state = [s_ref[h] for h in hs]
        r3 = [_dot(jnp.concatenate([uw[h][:, hd:], qs[h] * eg[h]], axis=0).astype(BF16),
                   state[h].astype(BF16)) for h in hs]
        vnb = [(uw[h][:, :hd] - r3[h][:chunk]).astype(BF16) for h in hs]
        o = [r3[h][chunk:] + _dot(attn[h].astype(BF16), vnb[h]) for h in hs]
        outs = []
        for h in hs:
            g_last = gcol[h][chunk - 1:chunk, :]
            s_ref[h] = state[h] * jnp.exp(g_last) + _dot(kdt[h], vnb[h])
            on = o[h] * lax.rsqrt(jnp.mean(o[h] * o[h], axis=-1, keepdims=True) + RMS_EPS) * gate_w
            outs.append(on * _silu(zs[h]))

        for s in range(nb):
            o_ref[s, pl.ds(r0, chunk), :] = jnp.concatenate(
                outs[s * heads:(s + 1) * heads], axis=-1).astype(o_ref.dtype)
        return carry

    lax.fori_loop(0, rows // chunk, chunk_body, 0)


def _gdn(proj, head_params, gate_w, rows=256, nb=4):
    heads, hd = GDN_HEADS, GDN_HEAD_DIM
    width = heads * hd
    batch, seq, _ = proj.shape
    ab_block = (4 * width + MEM_HEADS * MEM_HEAD_DIM) // LANES
    return pl.pallas_call(
        functools.partial(_gdn_kernel, heads=heads, hd=hd, chunk=GDN_CHUNK),
        grid=(batch // nb, seq // rows),
        in_specs=[
            pl.BlockSpec((nb, rows, 3 * width), lambda g, i: (g, i, 0)),
            pl.BlockSpec((nb, rows, width), lambda g, i: (g, i, 3)),
            pl.BlockSpec((nb, rows, LANES), lambda g, i: (g, i, ab_block)),
            _resident((2, LANES)),
            _resident((1, hd)),
        ],
        out_specs=pl.BlockSpec((nb, rows, width), lambda g, i: (g, i, 0)),
        out_shape=jax.ShapeDtypeStruct((batch, seq, width), BF16),
        scratch_shapes=[pltpu.VMEM((nb * heads, hd, hd), F32)],
        compiler_params=_params(2),
        name="gdn",
    )(proj, proj, proj, head_params, gate_w)


def _diff_attn_kernel(q_ref, k_ref, vt_ref, lp_ref, sw_ref, o_ref, s_scr, *, tq, tk, nh, lam_init):
    i = pl.program_id(2)
    d = DIFF_HEAD_DIM
    hw = 2 * d
    chains = range(nh)

    qs = []
    for c in chains:
        q = q_ref[:, c * hw:(c + 1) * hw]
        lane = lax.broadcasted_iota(jnp.int32, q.shape, 1)
        zero = jnp.zeros_like(q)
        qs.append(jnp.concatenate([jnp.where(lane < d, q, zero), jnp.where(lane >= d, q, zero)],
                                  axis=0))

    ones_rows = jnp.ones((BF16_ROWS, tk), BF16)

    units = [(c, n) for c in chains for n in range(2 * tq // tk)]
    lower = tuple(u for u, (_, n) in enumerate(units) if n % 2 == 0)
    upper = tuple(u for u, (_, n) in enumerate(units) if n % 2 == 1)
    everyone = lower + upper
    diag_ok = (lax.broadcasted_iota(jnp.int32, (tk, tk), 0)
               <= lax.broadcasted_iota(jnp.int32, (tk, tk), 1))

    def score_stage(j, buf, which):
        r0 = pl.multiple_of(j * tk, tk)
        block_max = [None] * len(units)
        for u in which:
            c, n = units[u]
            s = _dot_nt(k_ref[pl.ds(r0, tk), c * hw:(c + 1) * hw],
                        qs[c][n * tk:(n + 1) * tk])
            s_scr[buf, c, :, n * tk:(n + 1) * tk] = s
            block_max[u] = jnp.max(s, axis=0, keepdims=True)
        return tuple(block_max)

    def softmax_stage(j, buf, block_max, states, plain, diagonal=()):
        r0 = pl.multiple_of(j * tk, tk)
        out = list(states)
        for u in plain + diagonal:
            c, n = units[u]
            m, l, acc = states[u]
            s = s_scr[buf, c, :, n * tk:(n + 1) * tk]
            if u in diagonal:
                s = jnp.where(diag_ok, s, NEG_BIG)
                bm = jnp.max(s, axis=0, keepdims=True)
            else:
                bm = block_max[u]
            m_new = jnp.maximum(m, bm)
            a = jnp.exp2(m - m_new)
            p = jnp.exp2(s - m_new).astype(BF16)
            vt1 = jnp.concatenate([vt_ref[c * hw:(c + 1) * hw, pl.ds(r0, tk)], ones_rows], axis=0)
            pv = _dot(vt1, p)
            out[u] = (m_new, a * l + pv[hw:hw + 1], a * acc + pv[:hw])
        return tuple(out)

    def pair_body(t, carry):
        max_a, states = carry
        j = 2 * t
        max_b = score_stage(j + 1, 1, everyone)
        states = softmax_stage(j, 0, max_a, states, everyone)
        max_a = score_stage(j + 2, 0, everyone)
        states = softmax_stage(j + 1, 1, max_b, states, everyone)
        return max_a, states

    init = tuple((jnp.full((1, tk), NEG_BIG, F32), jnp.zeros((1, tk), F32),
                  jnp.zeros((hw, tk), F32)) for _ in units)
    max_a, states = lax.fori_loop(0, i, pair_body, (score_stage(0, 0, everyone), init))
    score_stage(2 * i + 1, 1, upper)
    states = softmax_stage(2 * i, 0, max_a, states, upper, lower)
    states = softmax_stage(2 * i + 1, 1, None, states, (), upper)

    lp = lp_ref[...]
    lam = (jnp.exp(jnp.sum(lp[0:1] * lp[1:2], axis=-1, keepdims=True))
           - jnp.exp(jnp.sum(lp[2:3] * lp[3:4], axis=-1, keepdims=True)) + lam_init)
    for c in chains:
        mine = [states[u] for u, (cc, _) in enumerate(units) if cc == c]
        l = jnp.concatenate([st[1] for st in mine], axis=1)
        acc = jnp.concatenate([st[2] for st in mine], axis=1)
        o = acc[:, :tq] / l[:, :tq] - lam * (acc[:, tq:] / l[:, tq:])
        on = o * lax.rsqrt(jnp.mean(o * o, axis=0, keepdims=True) + RMS_EPS) * sw_ref[...]
        o_ref[:, c * hw:(c + 1) * hw] = (on * (1.0 - lam_init)).T.astype(o_ref.dtype)


def _diff_attn(proj, v_t, lam_params, subln_w, batch, seq, lam_init, tq=1024, nh=1):
    heads = DIFF_HEADS
    hw = 2 * DIFF_HEAD_DIM
    tk = tq // 2
    n = batch * seq
    nq = seq // tq
    groups = heads // nh
    return pl.pallas_call(
        functools.partial(_diff_attn_kernel, tq=tq, tk=tk, nh=nh, lam_init=lam_init),
        grid=(batch, groups, nq),
        in_specs=[
            pl.BlockSpec((tq, nh * hw), lambda b, g, i: (b * nq + i, g)),
            pl.BlockSpec((seq, nh * hw), lambda b, g, i: (b, groups + g)),
            pl.BlockSpec((nh * hw, seq), lambda b, g, i: (b * groups + g, 0)),
            _resident((4, hw)),
            _resident((hw, 1)),
        ],
        out_specs=pl.BlockSpec((tq, nh * hw), lambda b, g, i: (b * nq + i, g)),
        out_shape=jax.ShapeDtypeStruct((n, heads * hw), BF16),
        scratch_shapes=[pltpu.VMEM((2, nh, tk, 2 * tq), F32)],
        compiler_params=_params(3),
        name="diff_attn",
    )(proj, proj, v_t, lam_params, subln_w)


def _out_proj_kernel(o_ref, mq_ref, kv_ref, w_ref, x_ref, g_ref, b_ref, out_ref):
    ow = o_ref.shape[1]
    mq = mq_ref[...].astype(BF16)
    kv = kv_ref[...]
    mw = MEM_HEADS * MEM_HEAD_DIM
    parts = []
    for h in range(MEM_HEADS):
        lo, hi = h * MEM_HEAD_DIM, (h + 1) * MEM_HEAD_DIM
        s = _dot_nt(mq[:, lo:hi], kv[:, lo:hi]) * (MEM_HEAD_DIM ** -0.5)
        p = jnp.exp(s - jnp.max(s, axis=-1, keepdims=True))
        pv = _dot(p.astype(BF16), kv[:, mw + lo:mw + hi])
        parts.append(pv / jnp.sum(p, axis=-1, keepdims=True))
    m = jnp.concatenate(parts, axis=-1).astype(BF16)
    mix = _dot(o_ref[...], w_ref[:ow, :].astype(BF16)) + _dot(m, w_ref[ow:, :].astype(BF16))
    y = ALPHA * x_ref[...] + mix
    out_ref[...] = _layer_norm(y, g_ref[...], b_ref[...])


def _out_proj(o, proj, mq_block, mem_kv, w, x2d, g, b, seq, mem_len, tm=512):
    n, d = x2d.shape
    mw = MEM_HEADS * MEM_HEAD_DIM
    per_batch = seq // tm
    return pl.pallas_call(
        _out_proj_kernel,
        grid=(n // tm,),
        in_specs=[
            pl.BlockSpec((tm, o.shape[1]), lambda i: (i, 0)),
            pl.BlockSpec((tm, mw), lambda i: (i, mq_block)),
            pl.BlockSpec((mem_len, 2 * mw), lambda i: (i // per_batch, 0)),
            _resident(w.shape),
            pl.BlockSpec((tm, d), lambda i: (i, 0)),
            _resident((1, d)),
            _resident((1, d)),
        ],
        out_specs=pl.BlockSpec((tm, d), lambda i: (i, 0)),
        out_shape=jax.ShapeDtypeStruct((n, d), F32),
        compiler_params=_params(1),
        name="out_proj",
    )(o, proj, mem_kv, w, x2d, g, b)


def _mlp_kernel(x_ref, w1_ref, w2_ref, g_ref, b_ref, out_ref, *, ff_chunk):
    x = x_ref[...]
    xb = x.astype(BF16)
    d_ff = w1_ref.shape[1]
    acc = jnp.zeros(x.shape, F32)
    for c0 in range(0, d_ff, ff_chunk):
        h = jnp.maximum(_dot(xb, w1_ref[:, c0:c0 + ff_chunk].astype(BF16)), 0.0)
        acc = acc + _dot((h * h).astype(BF16), w2_ref[c0:c0 + ff_chunk, :].astype(BF16))
    out_ref[...] = _layer_norm(ALPHA * x + acc, g_ref[...], b_ref[...])


def _mlp(x2d, w1, w2, g, b, tm=512, ff_chunk=512):
    n, d = x2d.shape
    return pl.pallas_call(
        functools.partial(_mlp_kernel, ff_chunk=ff_chunk),
        grid=(n // tm,),
        in_specs=[
            pl.BlockSpec((tm, d), lambda i: (i, 0)),
            _resident(w1.shape),
            _resident(w2.shape),
            _resident((1, d)),
            _resident((1, d)),
        ],
        out_specs=pl.BlockSpec((tm, d), lambda i: (i, 0)),
        out_shape=jax.ShapeDtypeStruct((n, d), F32),
        compiler_params=_params(1),
        name="mlp",
    )(x2d, w1, w2, g, b)


def _row(v):
    return v.reshape(1, -1).astype(F32)


def _pad_lanes(v, lanes=LANES):
    return jnp.pad(v.astype(F32), (0, lanes - v.shape[0]))


def kernel(x, mem, mem_ln_g, mem_ln_b, w_mem_kv, l0_w_in, l0_conv_w, l0_a_log, l0_dt_bias, l0_gate_norm_w, l0_w_out, l0_ln1_g, l0_ln1_b, l0_w_ff1, l0_w_ff2, l0_ln2_g, l0_ln2_b, l1_w_in, l1_lambda_q1, l1_lambda_k1, l1_lambda_q2, l1_lambda_k2, l1_subln_w, l1_w_out, l1_ln1_g, l1_ln1_b, l1_w_ff1, l1_w_ff2, l1_ln2_g, l1_ln2_b):
    batch, seq, d = x.shape
    mem_len = mem.shape[1]
    n = batch * seq
    gw = GDN_HEADS * GDN_HEAD_DIM
    dw = 2 * DIFF_HEADS * DIFF_HEAD_DIM
    mw = MEM_HEADS * MEM_HEAD_DIM

    x2d = x.reshape(n, d)
    mem_kv = _mem_kv(mem.reshape(batch * mem_len, d), _row(mem_ln_g), _row(mem_ln_b),
                     w_mem_kv, rows=mem_len)

    ab_cols = l0_w_in[:, 4 * gw:4 * gw + 2 * GDN_HEADS]
    w_in0 = jnp.concatenate(
        [l0_w_in[:, :4 * gw], l0_w_in[:, 4 * gw + 2 * GDN_HEADS:],
         jnp.pad(ab_cols, ((0, 0), (0, LANES - 2 * GDN_HEADS)))], axis=1).astype(BF16)
    proj0 = _in_proj_conv(x2d, w_in0, l0_conv_w.astype(F32), seq)
    head_params = jnp.stack([_pad_lanes(l0_a_log), _pad_lanes(l0_dt_bias)])
    o0 = _gdn(proj0.reshape(batch, seq, -1), head_params, _row(l0_gate_norm_w)).reshape(n, gw)
    x1 = _out_proj(o0, proj0, (4 * gw) // mw, mem_kv, l0_w_out, x2d,
                   _row(l0_ln1_g), _row(l0_ln1_b), seq, mem_len)
    x2 = _mlp(x1, l0_w_ff1, l0_w_ff2, _row(l0_ln2_g), _row(l0_ln2_b))

    lam_init = 0.8 - 0.6 * math.exp(-0.3 * 1)
    q_scale = DIFF_HEAD_DIM ** -0.5 * math.log2(math.e)
    proj1, v_t = _in_proj(x2, l1_w_in, seq, (2 * dw, 3 * dw), lead_cols=dw,
                          lead_scale=q_scale)
    lam_params = jnp.stack([_pad_lanes(v) for v in
                            (l1_lambda_q1, l1_lambda_k1, l1_lambda_q2, l1_lambda_k2)])
    o1 = _diff_attn(proj1, v_t.reshape(batch * dw, seq), lam_params,
                    l1_subln_w.reshape(-1, 1).astype(F32), batch, seq, lam_init)
    x3 = _out_proj(o1, proj1, (2 * dw) // mw, mem_kv, l1_w_out, x2,
                   _row(l1_ln1_g), _row(l1_ln1_b), seq, mem_len)
    x4 = _mlp(x3, l1_w_ff1, l1_w_ff2, _row(l1_ln2_g), _row(l1_ln2_b))
    return x4.reshape(batch, seq, d)
```
